```python
import math
import jax, jax.numpy as jnp
from jax import lax
import numpy as np

D_MODEL = 1024
BATCH = 4
SEQ = 4096
DEPTH = 4
DEC_BATCH = 16
DEC_SEQ = 4096
PAST_LEN = 128

GRID_W = 64
HEAD_DIM = D_MODEL // 16
NA_HEADS = 4
NA_KH_MAX = 8
NA_KW = 16
GQA_HEADS = 8
GQA_KV_HEADS = 2
Q_BLOCK = 128
ROPE_THETA = 10000.0
SG_GROUPS = 4
SG_CHUNK = 128
NA_WIDTH = NA_HEADS * HEAD_DIM
GQA_WIDTH = GQA_HEADS * HEAD_DIM
KV_WIDTH = GQA_KV_HEADS * HEAD_DIM
SG_WIDTH = SG_GROUPS * HEAD_DIM
D_MIX = NA_WIDTH + GQA_WIDTH + SG_WIDTH
D_IN = 3 * NA_WIDTH + GQA_WIDTH + 2 * KV_WIDTH + 2 * SG_WIDTH
N_EXPERTS = 16
CAPACITY_FACTOR = 2
D_EXPERT = 2048
LN_EPS = 1e-5
QK_EPS = 1e-6
DN_ALPHA = (2 * DEPTH) ** 0.25
DN_BETA = (8 * DEPTH) ** -0.25

kernel_name = "hybrid_bidir_na_gqa_sgu_ec_encoder"


def layer_norm(x, g, b):
    xf = x.astype(jnp.float32)
    mu = jnp.mean(xf, -1, keepdims=True)
    var = jnp.mean(jnp.square(xf - mu), -1, keepdims=True)
    return ((xf - mu) * lax.rsqrt(var + LN_EPS) * g + b).astype(x.dtype)


def rms_norm(x, g):
    xf = x.astype(jnp.float32)
    return (xf * lax.rsqrt(jnp.mean(xf * xf, -1, keepdims=True) + QK_EPS) * g).astype(x.dtype)


def axial_rope_angles(S):
    t = jnp.arange(S)
    row = (t // GRID_W).astype(jnp.float32)
    col = (t % GRID_W).astype(jnp.float32)
    n_freq = HEAD_DIM // 4
    inv = ROPE_THETA ** (-jnp.arange(n_freq, dtype=jnp.float32) / n_freq)
    ang = jnp.concatenate([row[:, None] * inv, col[:, None] * inv], -1)
    return jnp.cos(ang), jnp.sin(ang)


def apply_rope(x, cos, sin):
    xf = x.astype(jnp.float32).reshape(*x.shape[:-1], HEAD_DIM // 2, 2)
    x0, x1 = xf[..., 0], xf[..., 1]
    c = cos[None, :, None, :]
    s = sin[None, :, None, :]
    out = jnp.stack([x0 * c - x1 * s, x0 * s + x1 * c], -1)
    return out.reshape(x.shape).astype(x.dtype)


def neighbourhood_attention(q, k, v, rpb):
    B, S, H, dh = q.shape
    rows = S // GRID_W
    kh = min(NA_KH_MAX, rows)
    qg = q.reshape(B, rows, GRID_W, H, dh)
    kg = k.reshape(B, rows, GRID_W, H, dh)
    vg = v.reshape(B, rows, GRID_W, H, dh)
    col = jnp.arange(GRID_W)
    col_start = jnp.clip(col - NA_KW // 2, 0, GRID_W - NA_KW)
    col_idx = col_start[:, None] + jnp.arange(NA_KW)[None, :]
    dcol = col_idx - col[:, None] + (NA_KW - 1)
    scale = HEAD_DIM ** -0.5

    def row_block(r):
        rs = jnp.clip(r - kh // 2, 0, rows - kh)
        qr = lax.dynamic_index_in_dim(qg, r, axis=1, keepdims=False)
        kb = lax.dynamic_slice_in_dim(kg, rs, kh, axis=1)
        vb = lax.dynamic_slice_in_dim(vg, rs, kh, axis=1)
        kn = kb[:, :, col_idx]
        vn = vb[:, :, col_idx]
        drow = rs + jnp.arange(kh) - r + (NA_KH_MAX - 1)
        bias = rpb[:, drow][:, :, dcol].transpose(0, 2, 1, 3)
        s = jnp.einsum('bwhd,biwjhd->bhwij', qr, kn,
                       preferred_element_type=jnp.float32) * scale + bias[None].astype(jnp.float32)
        p = jax.nn.softmax(s.reshape(B, H, GRID_W, kh * NA_KW), axis=-1)
        p = p.reshape(B, H, GRID_W, kh, NA_KW).astype(vn.dtype)
        return jnp.einsum('bhwij,biwjhd->bwhd', p, vn)

    o = lax.map(row_block, jnp.arange(rows))
    return jnp.moveaxis(o, 0, 1).reshape(B, S, H, dh)


def gqa_attention(q, k, v):
    B, S, H, dh = q.shape
    G = H // GQA_KV_HEADS
    nb = S // Q_BLOCK
    scale = HEAD_DIM ** -0.5
    qb = q.reshape(B, nb, Q_BLOCK, GQA_KV_HEADS, G, dh).transpose(1, 0, 2, 3, 4, 5)

    def block(qi):
        s = jnp.einsum('bqkgd,bskd->bkgqs', qi, k, preferred_element_type=jnp.float32) * scale
        p = jax.nn.softmax(s, axis=-1).astype(v.dtype)
        return jnp.einsum('bkgqs,bskd->bqkgd', p, v)

    o = lax.map(block, qb)
    return o.transpose(1, 0, 2, 3, 4, 5).reshape(B, S, H, dh)


def spatial_gating(u, v, w_s, b_s, g_v, b_v):
    B, S, G, c = v.shape
    vn = layer_norm(v, g_v, b_v)
    nc = S // SG_CHUNK
    vc = vn.reshape(B, nc, SG_CHUNK, G, c)
    mixed = jnp.einsum('gpq,bnqgc->bnpgc', w_s, vc) + b_s.T[None, None, :, :, None]
    return u * mixed.reshape(B, S, G, c)


def token_mixer(x, w_in, rpb, q_norm, k_norm, w_s, b_s, g_v, b_v, w_out, cos, sin):
    B, S, _ = x.shape
    h = x @ w_in
    sizes = (NA_WIDTH, NA_WIDTH, NA_WIDTH, GQA_WIDTH, KV_WIDTH, KV_WIDTH, SG_WIDTH, SG_WIDTH)
    points = [int(p) for p in np.cumsum(sizes)[:-1]]
    qa, ka, va, qb, kb, vb, uc, vc = jnp.split(h, points, axis=-1)

    def heads(t, n):
        return t.reshape(B, S, n, HEAD_DIM)

    ya = neighbourhood_attention(heads(qa, NA_HEADS), heads(ka, NA_HEADS), heads(va, NA_HEADS), rpb)
    qb = apply_rope(rms_norm(heads(qb, GQA_HEADS), q_norm), cos, sin)
    kb = apply_rope(rms_norm(heads(kb, GQA_KV_HEADS), k_norm), cos, sin)
    yb = gqa_attention(qb, kb, heads(vb, GQA_KV_HEADS))
    yc = spatial_gating(jax.nn.gelu(heads(uc, SG_GROUPS)), jax.nn.gelu(heads(vc, SG_GROUPS)),
                        w_s, b_s, g_v, b_v)
    y = jnp.concatenate([ya.reshape(B, S, NA_WIDTH), yb.reshape(B, S, GQA_WIDTH),
                         yc.reshape(B, S, SG_WIDTH)], axis=-1)
    return y @ w_out


def expert_choice_ffn(x, w_router, w_gate, w_up, w_down):
    B, S, D = x.shape
    n = B * S
    cap = CAPACITY_FACTOR * n // N_EXPERTS
    xt = x.reshape(n, D)
    aff = jax.nn.softmax((xt @ w_router).astype(jnp.float32), axis=-1)
    gate, idx = lax.top_k(aff.T, cap)
    xe = xt[idx]
    hid = jax.nn.silu(jnp.einsum('ecd,edf->ecf', xe, w_gate)) * jnp.einsum('ecd,edf->ecf', xe, w_up)
    ye = jnp.einsum('ecf,efd->ecd', hid, w_down) * gate[..., None].astype(x.dtype)
    out = jnp.zeros_like(xt).at[idx.reshape(-1)].add(ye.reshape(-1, D))
    return out.reshape(B, S, D)


def run_trunk(x, ln_in_g, ln_in_b, w_in, na_rpb, q_norm, k_norm, sg_w, sg_b, sg_ln_g, sg_ln_b,
              w_out, ln1_g, ln1_b, w_router, w_gate, w_up, w_down, ln2_g, ln2_b):
    S = x.shape[1]
    cos, sin = axial_rope_angles(S)
    x = layer_norm(x, ln_in_g, ln_in_b)
    for l in range(DEPTH):
        mix = token_mixer(x, w_in[l], na_rpb[l], q_norm[l], k_norm[l], sg_w[l], sg_b[l],
                          sg_ln_g[l], sg_ln_b[l], w_out[l], cos, sin)
        x = layer_norm(DN_ALPHA * x + mix, ln1_g[l], ln1_b[l])
        ffn = expert_choice_ffn(x, w_router[l], w_gate[l], w_up[l], w_down[l])
        x = layer_norm(DN_ALPHA * x + ffn, ln2_g[l], ln2_b[l])
    return x


def setup_inputs(seed: int = 0) -> dict:
    key = jax.random.key(seed)
    ks = jax.random.split(key, 21)
    f32 = jnp.float32

    def nrm(k, shape, scale):
        return jax.random.normal(k, shape, f32) * scale

    def gain(k, shape):
        return 1.0 + 0.02 * jax.random.normal(k, shape, f32)

    return {
        "x_prompt": nrm(ks[0], (BATCH, SEQ, D_MODEL), 1.0),
        "x_sample": nrm(ks[1], (DEC_BATCH, DEC_SEQ, D_MODEL), 1.0),
        "ln_in_g": gain(ks[2], (D_MODEL,)),
        "ln_in_b": nrm(ks[3], (D_MODEL,), 0.02),
        "w_in": nrm(ks[4], (DEPTH, D_MODEL, D_IN), D_MODEL ** -0.5),
        "na_rpb": nrm(ks[5], (DEPTH, NA_HEADS, 2 * NA_KH_MAX - 1, 2 * NA_KW - 1), 0.1),
        "q_norm": gain(ks[6], (DEPTH, HEAD_DIM)),
        "k_norm": gain(ks[7], (DEPTH, HEAD_DIM)),
        "sg_w": nrm(ks[8], (DEPTH, SG_GROUPS, SG_CHUNK, SG_CHUNK), SG_CHUNK ** -0.5),
        "sg_b": gain(ks[9], (DEPTH, SG_GROUPS, SG_CHUNK)),
        "sg_ln_g": gain(ks[10], (DEPTH, SG_GROUPS, HEAD_DIM)),
        "sg_ln_b": nrm(ks[11], (DEPTH, SG_GROUPS, HEAD_DIM), 0.02),
        "w_out": nrm(ks[12], (DEPTH, D_MIX, D_MODEL), DN_BETA * D_MIX ** -0.5),
        "ln1_g": gain(ks[13], (DEPTH, D_MODEL)),
        "ln1_b": nrm(ks[14], (DEPTH, D_MODEL), 0.02),
        "w_router": nrm(ks[15], (DEPTH, D_MODEL, N_EXPERTS), D_MODEL ** -0.5),
        "w_gate": nrm(ks[16], (DEPTH, N_EXPERTS, D_MODEL, D_EXPERT), D_MODEL ** -0.5),
        "w_up": nrm(ks[17], (DEPTH, N_EXPERTS, D_MODEL, D_EXPERT), D_MODEL ** -0.5),
        "w_down": nrm(ks[18], (DEPTH, N_EXPERTS, D_EXPERT, D_MODEL), DN_BETA * D_EXPERT ** -0.5),
        "ln2_g": gain(ks[19], (DEPTH, D_MODEL)),
        "ln2_b": nrm(ks[20], (DEPTH, D_MODEL), 0.02),
    }


def reference(x_prompt, x_sample, ln_in_g, ln_in_b, w_in, na_rpb, q_norm, k_norm, sg_w, sg_b,
              sg_ln_g, sg_ln_b, w_out, ln1_g, ln1_b, w_router, w_gate, w_up, w_down, ln2_g, ln2_b):
    y_prompt = run_trunk(x_prompt, ln_in_g, ln_in_b, w_in, na_rpb, q_norm, k_norm, sg_w, sg_b,
                         sg_ln_g, sg_ln_b, w_out, ln1_g, ln1_b, w_router, w_gate, w_up, w_down,
                         ln2_g, ln2_b)
    y_sample = run_trunk(x_sample, ln_in_g, ln_in_b, w_in, na_rpb, q_norm, k_norm, sg_w, sg_b,
                         sg_ln_g, sg_ln_b, w_out, ln1_g, ln1_b, w_router, w_gate, w_up, w_down,
                         ln2_g, ln2_b)
    return (y_prompt, y_sample)
```

```python
import functools

import numpy as np
import jax
import jax.numpy as jnp
from jax import lax
from jax.experimental import pallas as pl
from jax.experimental.pallas import tpu as pltpu

f32 = jnp.float32
bf16 = jnp.bfloat16
i32 = jnp.int32

D_MODEL = 1024
SEQ = 4096
DEPTH = 4
GRID_W = 64
GRID_ROWS = SEQ // GRID_W
HEAD_DIM = 64
NA_HEADS = 4
NA_KH = 8
NA_KW = 16
GQA_HEADS = 8
GQA_KV = 2
ROPE_THETA = 10000.0
SG_GROUPS = 4
SG_CHUNK = 128
N_EXPERTS = 16
D_EXPERT = 2048
LN_EPS = 1e-5
QK_EPS = 1e-6
DN_ALPHA = (2 * DEPTH) ** 0.25
QK_SCALE = HEAD_DIM ** -0.5

LANES = 128
NEG = -1e30
MIB = 1024 * 1024

NA_ROWS_PER_STEP = 4
GQA_TQ = 128
GQA_TK = 512
ROW_TILE = 512
FFN_TM = 256
FFN_FCHUNK = 512
ROUTE_PW = 512
COMBINE_TT = 256
COMBINE_RC = 256
SMEM_INDEX_BYTES = 128 * 1024


def _cp(semantics, vmem_mib=48):
    return pltpu.CompilerParams(dimension_semantics=semantics, vmem_limit_bytes=vmem_mib * MIB)


def _dot(a, b):
    return jnp.dot(a, b, preferred_element_type=f32)


def _dot_nt(a, b):
    return lax.dot_general(a, b, (((1,), (1,)), ((), ())), preferred_element_type=f32)


def _layer_norm(x, g, b):
    mu = jnp.mean(x, -1, keepdims=True)
    xc = x - mu
    var = jnp.mean(xc * xc, -1, keepdims=True)
    return xc * lax.rsqrt(var + LN_EPS) * g + b


def _split2(x):
    hi = x.astype(bf16)
    lo = (x - hi.astype(f32)).astype(bf16)
    return hi, lo


def _seg_mean(x, bd):
    hi, lo = _split2(x)
    return _dot(hi, bd) + _dot(lo, bd)


def _ln_in_kernel(x_ref, g_ref, b_ref, o_ref):
    o_ref[...] = _layer_norm(x_ref[...], g_ref[...], b_ref[...])


def _ln_in(x, g, b):
    n = x.shape[0]
    tm = 1024
    row = pl.BlockSpec((tm, D_MODEL), lambda i: (i, 0))
    vec = pl.BlockSpec((1, D_MODEL), lambda i: (0, 0))
    return pl.pallas_call(
        _ln_in_kernel, grid=(n // tm,), in_specs=[row, vec, vec], out_specs=row,
        out_shape=jax.ShapeDtypeStruct((n, D_MODEL), f32), compiler_params=_cp(("parallel",)),
    )(x, g.reshape(1, -1), b.reshape(1, -1))


def _proj_kernel(x_ref, w_ref, cos_ref, sin_ref, qg_ref, kg_ref, bd_ref,
                 hna_ref, qn_ref, kn_ref, vb_ref, hsg_ref):
    tm = x_ref.shape[0]
    xb = x_ref[...].astype(bf16)

    def mm(c):
        return _dot(xb, w_ref[:, 256 * c:256 * (c + 1)])

    hna_ref[:, 0:256] = (mm(0) * QK_SCALE).astype(bf16)
    hna_ref[:, 256:512] = mm(1).astype(bf16)
    hna_ref[:, 512:768] = mm(2).astype(bf16)

    cos = cos_ref[...]
    sin = sin_ref[...]
    bd = bd_ref[...]
    even = (lax.broadcasted_iota(i32, (tm, LANES), 1) & 1) == 0

    def norm_rope(x, g):
        y = x * lax.rsqrt(_seg_mean(x * x, bd) + QK_EPS) * g
        partner = jnp.where(even, pltpu.roll(y, LANES - 1, 1), pltpu.roll(y, 1, 1))
        return y * cos + partner * sin

    qg = qg_ref[...]
    for c in (3, 4):
        a = mm(c)
        for j in range(2):
            col = (c - 3) * 256 + j * LANES
            qn_ref[:, col:col + LANES] = (
                norm_rope(a[:, j * LANES:(j + 1) * LANES], qg) * QK_SCALE).astype(bf16)
    a = mm(5)
    kn = norm_rope(a[:, :LANES], kg_ref[...]).astype(bf16)
    kn_ref[0, 0] = kn[:, :HEAD_DIM]
    kn_ref[0, 1] = kn[:, HEAD_DIM:]
    vb = a[:, LANES:].astype(bf16)
    vb_ref[0, 0] = vb[:, :HEAD_DIM]
    vb_ref[0, 1] = vb[:, HEAD_DIM:]
    hsg_ref[:, :256] = mm(6)
    hsg_ref[:, 256:] = mm(7)


def _proj(x, w, cos_t, sin_t, qg, kg, bd, batch):
    n = x.shape[0]
    tm = ROW_TILE
    per_seq = SEQ // tm
    row = lambda width: pl.BlockSpec((tm, width), lambda i: (i, 0))
    const = lambda shape: pl.BlockSpec(shape, lambda i: tuple(0 for _ in shape))
    tab = pl.BlockSpec((tm, LANES), lambda i: (i % per_seq, 0))
    head_major = pl.BlockSpec((1, GQA_KV, tm, HEAD_DIM), lambda i: (i // per_seq, 0, i % per_seq, 0))
    return pl.pallas_call(
        _proj_kernel, grid=(n // tm,),
        in_specs=[row(D_MODEL), const(w.shape), tab, tab, const((1, LANES)), const((1, LANES)),
                  const((LANES, LANES))],
        out_specs=[row(768), row(512), head_major, head_major, row(512)],
        out_shape=[jax.ShapeDtypeStruct((n, 768), bf16),
                   jax.ShapeDtypeStruct((n, 512), bf16),
                   jax.ShapeDtypeStruct((batch, GQA_KV, SEQ, HEAD_DIM), bf16),
                   jax.ShapeDtypeStruct((batch, GQA_KV, SEQ, HEAD_DIM), bf16),
                   jax.ShapeDtypeStruct((n, 512), f32)],
        compiler_params=_cp(("parallel",)),
    )(x, w, cos_t, sin_t, qg, kg, bd)


def _na_kernel(q_ref, k_ref, v_ref, bias_ref, o_ref):
    r0 = pl.program_id(1) * NA_ROWS_PER_STEP
    band = NA_KH * GRID_W

    def body(rr, carry):
        r = r0 + rr
        rs = jnp.clip(r - NA_KH // 2, 0, GRID_ROWS - NA_KH)
        d = r - rs
        qrow = pl.multiple_of(rr * GRID_W, GRID_W)
        krow = pl.multiple_of(rs * GRID_W, GRID_W)
        q = q_ref[0, pl.ds(qrow, GRID_W), :]
        kb = k_ref[0, pl.ds(krow, band), :]
        vb = v_ref[0, pl.ds(krow, band), :]
        outs = []
        for h in range(NA_HEADS):
            cs = slice(h * HEAD_DIM, (h + 1) * HEAD_DIM)
            s = _dot_nt(q[:, cs], kb[:, cs]) + bias_ref[d, h]
            m = jnp.max(s, -1, keepdims=True)
            e = jnp.exp(s - m)
            l = jnp.sum(e, -1, keepdims=True)
            outs.append(_dot(e.astype(bf16), vb[:, cs]) / l)
        o_ref[0, pl.ds(qrow, GRID_W), :] = jnp.concatenate(outs, -1).astype(bf16)
        return carry

    lax.fori_loop(0, NA_ROWS_PER_STEP, body, 0)


def _na(hna, bias_tab):
    batch = hna.shape[0]
    tq = NA_ROWS_PER_STEP * GRID_W
    width = NA_HEADS * HEAD_DIM
    return pl.pallas_call(
        _na_kernel, grid=(batch, SEQ // tq),
        in_specs=[pl.BlockSpec((1, tq, width), lambda b, i: (b, i, 0)),
                  pl.BlockSpec((1, SEQ, width), lambda b, i: (b, 0, 1)),
                  pl.BlockSpec((1, SEQ, width), lambda b, i: (b, 0, 2)),
                  pl.BlockSpec(bias_tab.shape, lambda b, i: (0, 0, 0, 0))],
        out_specs=pl.BlockSpec((1, tq, width), lambda b, i: (b, i, 0)),
        out_shape=jax.ShapeDtypeStruct((batch, SEQ, width), bf16),
        compiler_params=_cp(("parallel", "arbitrary")),
    )(hna, hna, hna, bias_tab)


def _na_bias_table(rpb):
    col = np.arange(GRID_W)
    col_start = np.clip(col - NA_KW // 2, 0, GRID_W - NA_KW)
    kc = np.arange(GRID_W)
    valid = (kc[None, :] >= col_start[:, None]) & (kc[None, :] < col_start[:, None] + NA_KW)
    dcol = np.clip(kc[None, :] - col[:, None] + NA_KW - 1, 0, 2 * NA_KW - 2)
    tabs = []
    for d in range(NA_KH):
        drow = np.arange(NA_KH) - d + NA_KH - 1
        g = rpb[:, drow][:, :, dcol]
        g = jnp.where(valid[None, None], g, NEG)
        tabs.append(jnp.transpose(g, (0, 2, 1, 3)).reshape(NA_HEADS, GRID_W, NA_KH * GRID_W))
    return jnp.stack(tabs).astype(f32)


def _gqa_kernel(q_ref, k_ref, v_ref, o_ref):
    g_per_kv = GQA_HEADS // GQA_KV
    q = q_ref[0]
    q4 = jnp.concatenate([q[:, g * HEAD_DIM:(g + 1) * HEAD_DIM] for g in range(g_per_kv)], axis=0)
    rows = g_per_kv * GQA_TQ

    def body(j, carry):
        m, l, acc = carry
        start = pl.multiple_of(j * GQA_TK, GQA_TK)
        kc = k_ref[0, 0, pl.ds(start, GQA_TK), :]
        vc = v_ref[0, 0, pl.ds(start, GQA_TK), :]
        s = _dot_nt(q4, kc)
        mn = jnp.maximum(m, jnp.max(s, -1, keepdims=True))
        alpha = jnp.exp(m - mn)
        p = jnp.exp(s - mn)
        l = alpha * l + jnp.sum(p, -1, keepdims=True)
        acc = alpha * acc + _dot(p.astype(bf16), vc)
        return mn, l, acc

    init = (jnp.full((rows, 1), NEG, f32), jnp.zeros((rows, 1), f32), jnp.zeros((rows, HEAD_DIM), f32))
    _, l, acc = lax.fori_loop(0, SEQ // GQA_TK, body, init)
    o = acc / l
    for g in range(g_per_kv):
        o_ref[0, :, g * HEAD_DIM:(g + 1) * HEAD_DIM] = o[g * GQA_TQ:(g + 1) * GQA_TQ].astype(bf16)


def _gqa(qn, kn, vb):
    batch = qn.shape[0]
    width = (GQA_HEADS // GQA_KV) * HEAD_DIM
    qspec = pl.BlockSpec((1, GQA_TQ, width), lambda b, kv, i: (b, i, kv))
    kvspec = pl.BlockSpec((1, 1, SEQ, HEAD_DIM), lambda b, kv, i: (b, kv, 0, 0))
    return pl.pallas_call(
        _gqa_kernel, grid=(batch, GQA_KV, SEQ // GQA_TQ),
        in_specs=[qspec, kvspec, kvspec], out_specs=qspec,
        out_shape=jax.ShapeDtypeStruct((batch, SEQ, GQA_HEADS * HEAD_DIM), bf16),
        compiler_params=_cp(("parallel", "parallel", "arbitrary")),
    )(qn, kn, vb)


def _sgu_kernel(h_ref, ws_ref, bs_ref, g_ref, b_ref, bd_ref, o_ref):
    tm = h_ref.shape[0]
    width = SG_GROUPS * HEAD_DIM
    bd = bd_ref[...]
    u = jax.nn.gelu(h_ref[:, :width])
    v = jax.nn.gelu(h_ref[:, width:])
    parts = []
    for j in range(width // LANES):
        x = v[:, j * LANES:(j + 1) * LANES]
        xc = x - _seg_mean(x, bd)
        parts.append(xc * lax.rsqrt(_seg_mean(xc * xc, bd) + LN_EPS))
    vn = (jnp.concatenate(parts, 1) * g_ref[...] + b_ref[...]).astype(bf16)
    bs = bs_ref[...]
    for c in range(tm // SG_CHUNK):
        rows = slice(c * SG_CHUNK, (c + 1) * SG_CHUNK)
        mixed = jnp.concatenate(
            [_dot(ws_ref[g], vn[rows, g * HEAD_DIM:(g + 1) * HEAD_DIM]) for g in range(SG_GROUPS)], axis=1)
        o_ref[rows, :] = (u[rows] * (mixed + bs)).astype(bf16)


def _sgu(hsg, ws, bs_tab, g, b, bd):
    n = hsg.shape[0]
    tm = ROW_TILE
    width = SG_GROUPS * HEAD_DIM
    const = lambda shape: pl.BlockSpec(shape, lambda i: tuple(0 for _ in shape))
    return pl.pallas_call(
        _sgu_kernel, grid=(n // tm,),
        in_specs=[pl.BlockSpec((tm, 2 * width), lambda i: (i, 0)), const(ws.shape), const(bs_tab.shape),
                  const((1, width)), const((1, width)), const((LANES, LANES))],
        out_specs=pl.BlockSpec((tm, width), lambda i: (i, 0)),
        out_shape=jax.ShapeDtypeStruct((n, width), bf16),
        compiler_params=_cp(("parallel",)),
    )(hsg, ws, bs_tab, g, b, bd)


def _outproj_kernel(ya_ref, yb_ref, yc_ref, x_ref, w_ref, g_ref, b_ref, wrh_ref, wrl_ref, x1_ref, aff_ref):
    mix = (_dot(ya_ref[...], w_ref[0:256]) + _dot(yb_ref[...], w_ref[256:768])
           + _dot(yc_ref[...], w_ref[768:1024]))
    x1 = _layer_norm(DN_ALPHA * x_ref[...] + mix, g_ref[...], b_ref[...])
    x1_ref[...] = x1
    hi, lo = _split2(x1)
    wrh = wrh_ref[...]
    logits = _dot(hi, wrh) + _dot(lo, wrh) + _dot(hi, wrl_ref[...])
    m = jnp.max(logits, -1, keepdims=True)
    e = jnp.exp(logits - m)
    aff_ref[...] = e / jnp.sum(e, -1, keepdims=True)


def _outproj(ya, yb, yc, x, w, g, b, wrh, wrl):
    n = x.shape[0]
    tm = ROW_TILE
    row = lambda width: pl.BlockSpec((tm, width), lambda i: (i, 0))
    const = lambda shape: pl.BlockSpec(shape, lambda i: tuple(0 for _ in shape))
    return pl.pallas_call(
        _outproj_kernel, grid=(n // tm,),
        in_specs=[row(256), row(512), row(256), row(D_MODEL), const(w.shape), const((1, D_MODEL)),
                  const((1, D_MODEL)), const(wrh.shape), const(wrl.shape)],
        out_specs=[row(D_MODEL), row(N_EXPERTS)],
        out_shape=[jax.ShapeDtypeStruct((n, D_MODEL), f32), jax.ShapeDtypeStruct((n, N_EXPERTS), f32)],
        compiler_params=_cp(("parallel",)),
    )(ya, yb, yc, x, w, g, b, wrh, wrl)


def _tri_incl():
    r = lax.broadcasted_iota(i32, (LANES, LANES), 0)
    c = lax.broadcasted_iota(i32, (LANES, LANES), 1)
    return (r <= c).astype(bf16)


def _tri_strict_lower(nc):
    r = lax.broadcasted_iota(i32, (nc, nc), 0)
    c = lax.broadcasted_iota(i32, (nc, nc), 1)
    return (c < r).astype(bf16)


def _byte_planes(x, planes):
    xi = x.astype(i32)
    return [((xi >> (8 * k)) & 255).astype(f32).astype(bf16) for k in range(planes)]


def _token_prefix(m, planes):
    nc = m.shape[0]
    lincl = _dot(m.astype(bf16), _tri_incl())
    tot = jnp.broadcast_to(lincl[:, LANES - 1:LANES], m.shape)
    sl = _tri_strict_lower(nc)
    pref = sum(_dot(sl, p) * float(256 ** k) for k, p in enumerate(_byte_planes(tot, planes)))
    return lincl, tot, pref


def _route1_kernel(aff_ref, sel_ref, *, cap):
    bits = lax.bitcast_convert_type(aff_ref[0], i32)

    def body(i, t):
        cand = t | lax.shift_left(jnp.int32(1), 30 - i)
        cnt = jnp.sum((bits >= cand).astype(f32))
        return jnp.where(cnt >= cap, cand, t)

    thr = lax.fori_loop(0, 31, body, jnp.int32(0))
    gt = bits > thr
    eq = bits == thr
    need = cap - jnp.sum(gt.astype(f32))
    eqf = eq.astype(f32)
    lincl, _, pref = _token_prefix(eqf, 1)
    before = pref + lincl - eqf
    sel_ref[0] = (gt | (eq & (before < need))).astype(f32)


def _route1(aff_t, cap):
    ne, nc, _ = aff_t.shape
    spec = pl.BlockSpec((1, nc, LANES), lambda e: (e, 0, 0))
    return pl.pallas_call(
        functools.partial(_route1_kernel, cap=cap), grid=(ne,), in_specs=[spec], out_specs=spec,
        out_shape=jax.ShapeDtypeStruct(aff_t.shape, f32), compiler_params=_cp(("parallel",)),
    )(aff_t)


def _route2_kernel(aff_ref, sel_ref, idx_ref, gate_ref, rho_ref, off_ref, cnt_ref, *, cap):
    e = pl.program_id(0)
    nc = aff_ref.shape[1]
    pw = ROUTE_PW

    count = jnp.zeros((nc, LANES), f32)
    rank = jnp.zeros((nc, LANES), f32)
    for k in range(N_EXPERTS):
        sk = sel_ref[k]
        count = count + sk
        rank = rank + jnp.where(k < e, sk, 0.0)
    lc, _, pc = _token_prefix(count, 2)
    off = pc + lc - count
    off_ref[...] = off
    cnt_ref[...] = count

    sel = sel_ref[e]
    lincl, tot, pref = _token_prefix(sel, 1)
    cse = pref[:, 0:1]
    csi = cse + tot[:, 0:1]
    chunk_id = lax.broadcasted_iota(i32, (nc, 1), 0).astype(f32)

    lt = lincl.T.astype(bf16)
    a = aff_ref[0].T
    a1 = a.astype(bf16)
    r1 = a - a1.astype(f32)
    a2 = r1.astype(bf16)
    a3 = (r1 - a2.astype(f32)).astype(bf16)
    dest = _byte_planes((off + rank).T, 3)
    lane_id = lax.broadcasted_iota(i32, (LANES, pw), 0).astype(f32)

    def tile(k, carry):
        p = (k * pw + lax.broadcasted_iota(i32, (1, pw), 1)).astype(f32)
        oh = (cse <= p) & (p < csi)
        ohb = oh.astype(f32).astype(bf16)
        chunk = jnp.sum(jnp.where(oh, chunk_id, 0.0), axis=0, keepdims=True)
        local = p - jnp.sum(jnp.where(oh, cse, 0.0), axis=0, keepdims=True)
        jstar = jnp.sum((_dot(lt, ohb) <= local).astype(f32), axis=0, keepdims=True)
        hit = lane_id == jstar
        g = _dot(a1, ohb) + _dot(a2, ohb) + _dot(a3, ohb)
        r = _dot(dest[0], ohb) + 256.0 * _dot(dest[1], ohb) + 65536.0 * _dot(dest[2], ohb)
        idx_ref[0, pl.ds(k, 1), :] = (chunk * LANES + jstar).astype(i32)
        gate_ref[0, pl.ds(k, 1), :] = jnp.sum(jnp.where(hit, g, 0.0), axis=0, keepdims=True)
        rho_ref[0, pl.ds(k, 1), :] = jnp.sum(jnp.where(hit, r, 0.0), axis=0, keepdims=True).astype(i32)
        return carry

    lax.fori_loop(0, cap // pw, tile, 0)


def _route2(aff_t, sel, cap):
    ne, nc, _ = aff_t.shape
    pw = ROUTE_PW
    slot = pl.BlockSpec((1, cap // pw, pw), lambda e: (e, 0, 0))
    tok = pl.BlockSpec((nc, LANES), lambda e: (0, 0))
    slot_shape = lambda dt: jax.ShapeDtypeStruct((ne, cap // pw, pw), dt)
    return pl.pallas_call(
        functools.partial(_route2_kernel, cap=cap), grid=(ne,),
        in_specs=[pl.BlockSpec((1, nc, LANES), lambda e: (e, 0, 0)),
                  pl.BlockSpec((ne, nc, LANES), lambda e: (0, 0, 0))],
        out_specs=[slot, slot, slot, tok, tok],
        out_shape=[slot_shape(i32), slot_shape(f32), slot_shape(i32),
                   jax.ShapeDtypeStruct((nc, LANES), f32), jax.ShapeDtypeStruct((nc, LANES), f32)],
        compiler_params=_cp(("arbitrary",)),
    )(aff_t, sel)


def _row_to_col(row):
    n = row.shape[1]
    r = lax.broadcasted_iota(i32, (n, n), 0)
    c = lax.broadcasted_iota(i32, (n, n), 1)
    return jnp.sum(jnp.where(r == c, row, 0.0), axis=1, keepdims=True)


def _ffn_kernel(idx_sm, rho_sm, x_hbm, gate_ref, wg_ref, wu_ref, wd_ref, *rest, aliased):
    if aliased:
        _, z_hbm, xbuf, ybuf, gsem, ssem = rest
    else:
        z_hbm, xbuf, ybuf, gsem, ssem = rest
    tm = FFN_TM
    mt = pl.num_programs(1)
    total = pl.num_programs(0) * mt
    s = pl.program_id(0) * mt + pl.program_id(1)
    slot = s % 2

    def gather_copy(tok, i, sl):
        return pltpu.make_async_copy(x_hbm.at[pl.ds(tok, 1), :], xbuf.at[sl, pl.ds(i, 1), :], gsem.at[sl])

    def scatter_copy(row, i, sl):
        return pltpu.make_async_copy(ybuf.at[sl, pl.ds(i, 1), :], z_hbm.at[pl.ds(row, 1), :], ssem.at[sl])

    def start_gather(step, sl):
        def body(i, c):
            gather_copy(idx_sm[step * tm + i], i, sl).start()
            return c
        lax.fori_loop(0, tm, body, 0)

    def wait_rows(make, sl):
        def body(i, c):
            make(0, i, sl).wait()
            return c
        lax.fori_loop(0, tm, body, 0)

    @pl.when(s == 0)
    def _():
        start_gather(0, 0)

    @pl.when(s + 1 < total)
    def _():
        start_gather(s + 1, 1 - slot)

    wait_rows(gather_copy, slot)

    @pl.when(s >= 2)
    def _():
        wait_rows(scatter_copy, slot)

    x = xbuf[slot].astype(bf16)
    y = jnp.zeros((tm, D_MODEL), f32)
    for c in range(D_EXPERT // FFN_FCHUNK):
        cols = slice(c * FFN_FCHUNK, (c + 1) * FFN_FCHUNK)
        hg = _dot(x, wg_ref[0, :, cols])
        hu = _dot(x, wu_ref[0, :, cols])
        y = y + _dot((jax.nn.silu(hg) * hu).astype(bf16), wd_ref[0, cols, :])
    ybuf[slot] = y * _row_to_col(gate_ref[0])

    def start_scatter(i, c):
        scatter_copy(rho_sm[s * tm + i], i, slot).start()
        return c
    lax.fori_loop(0, tm, start_scatter, 0)

    @pl.when(s == total - 1)
    def _():
        wait_rows(scatter_copy, slot)

        @pl.when(total >= 2)
        def _():
            wait_rows(scatter_copy, 1 - slot)


def _ffn_call(idx, rho, x1, gate, wg, wu, wd, z, e0, eg, cap):
    tm = FFN_TM
    mt = cap // tm
    n_rows = N_EXPERTS * cap
    aliased = z is not None
    any_spec = pl.BlockSpec(memory_space=pl.ANY)
    in_specs = [any_spec,
                pl.BlockSpec((1, 1, tm), lambda e, m, *_: ((e0 + e) * mt + m, 0, 0)),
                pl.BlockSpec((1, D_MODEL, D_EXPERT), lambda e, m, *_: (e0 + e, 0, 0)),
                pl.BlockSpec((1, D_MODEL, D_EXPERT), lambda e, m, *_: (e0 + e, 0, 0)),
                pl.BlockSpec((1, D_EXPERT, D_MODEL), lambda e, m, *_: (e0 + e, 0, 0))]
    args = [x1, gate, wg, wu, wd]
    if aliased:
        in_specs.append(any_spec)
        args.append(z)
    grid_spec = pltpu.PrefetchScalarGridSpec(
        num_scalar_prefetch=2, grid=(eg, mt), in_specs=in_specs, out_specs=any_spec,
        scratch_shapes=[pltpu.VMEM((2, tm, D_MODEL), f32), pltpu.VMEM((2, tm, D_MODEL), f32),
                        pltpu.SemaphoreType.DMA((2,)), pltpu.SemaphoreType.DMA((2,))])
    return pl.pallas_call(
        functools.partial(_ffn_kernel, aliased=aliased), grid_spec=grid_spec,
        out_shape=jax.ShapeDtypeStruct((n_rows, D_MODEL), f32),
        input_output_aliases={7: 0} if aliased else {},
        compiler_params=pltpu.CompilerParams(dimension_semantics=("arbitrary", "arbitrary"),
                                             vmem_limit_bytes=56 * MIB, has_side_effects=True),
    )(idx[e0:e0 + eg].reshape(-1), rho[e0:e0 + eg].reshape(-1), *args)


def _ffn(idx, rho, x1, gate, wg, wu, wd, cap):
    eg = max(1, min(N_EXPERTS, SMEM_INDEX_BYTES // (2 * 4 * cap)))
    gate3 = gate.reshape(N_EXPERTS * cap // FFN_TM, 1, FFN_TM)
    z = None
    for e0 in range(0, N_EXPERTS, eg):
        z = _ffn_call(idx, rho, x1, gate3, wg, wu, wd, z, e0, eg, cap)
    return z


def _combine_kernel(rs_sm, x_ref, off_ref, cnt_ref, g_ref, b_ref, z_hbm, o_ref, zbuf, sem, *, total_rows):
    tt = COMBINE_TT
    rc = COMBINE_RC
    i = pl.program_id(0)
    rstart = rs_sm[i]
    rend = rs_sm[i + 1]
    r0 = (rstart >> 3) << 3
    nch = (rend - r0 + rc - 1) // rc

    offs = []
    ends = []
    for c in range(tt // LANES):
        o_row = off_ref[0, c:c + 1, :]
        offs.append(_row_to_col(o_row))
        ends.append(_row_to_col(o_row + cnt_ref[0, c:c + 1, :]))
    off_col = jnp.concatenate(offs, 0)
    end_col = jnp.concatenate(ends, 0)

    def chunk_start(j):
        return pl.multiple_of(jnp.minimum(r0 + j * rc, total_rows - rc), 8)

    def chunk_copy(j, sl):
        return pltpu.make_async_copy(z_hbm.at[pl.ds(chunk_start(j), rc), :], zbuf.at[sl], sem.at[sl])

    @pl.when(nch > 0)
    def _():
        chunk_copy(0, 0).start()

    def body(j, acc):
        sl = j % 2

        @pl.when(j + 1 < nch)
        def _():
            chunk_copy(j + 1, 1 - sl).start()

        chunk_copy(j, sl).wait()
        rowid = chunk_start(j) + lax.broadcasted_iota(i32, (1, rc), 1)
        fresh = rowid >= r0 + j * rc
        rowf = rowid.astype(f32)
        seg = ((off_col <= rowf) & (rowf < end_col) & fresh).astype(f32).astype(bf16)
        hi, lo = _split2(zbuf[sl])
        return acc + _dot(seg, hi) + _dot(seg, lo)

    acc = lax.fori_loop(0, nch, body, jnp.zeros((tt, D_MODEL), f32))
    o_ref[...] = _layer_norm(DN_ALPHA * x_ref[...] + acc, g_ref[...], b_ref[...])


def _combine(row_starts, x1, off, cnt, g, b, z):
    n = x1.shape[0]
    tt = COMBINE_TT
    per = tt // LANES
    tok = pl.BlockSpec((1, per, LANES), lambda i, *_: (i, 0, 0))
    vec = pl.BlockSpec((1, D_MODEL), lambda i, *_: (0, 0))
    row = pl.BlockSpec((tt, D_MODEL), lambda i, *_: (i, 0))
    grid_spec = pltpu.PrefetchScalarGridSpec(
        num_scalar_prefetch=1, grid=(n // tt,),
        in_specs=[row, tok, tok, vec, vec, pl.BlockSpec(memory_space=pl.ANY)], out_specs=row,
        scratch_shapes=[pltpu.VMEM((2, COMBINE_RC, D_MODEL), f32), pltpu.SemaphoreType.DMA((2,))])
    return pl.pallas_call(
        functools.partial(_combine_kernel, total_rows=z.shape[0]), grid_spec=grid_spec,
        out_shape=jax.ShapeDtypeStruct((n, D_MODEL), f32), compiler_params=_cp(("arbitrary",)),
    )(row_starts, x1, off.reshape(n // tt, per, LANES), cnt.reshape(n // tt, per, LANES), g, b, z)


def _rope_tables():
    t = np.arange(SEQ)
    row = (t // GRID_W).astype(np.float32)
    col = (t % GRID_W).astype(np.float32)
    n_freq = HEAD_DIM // 4
    inv = jnp.asarray(ROPE_THETA, f32) ** (-jnp.arange(n_freq, dtype=f32) / n_freq)
    ang = jnp.concatenate([row[:, None] * inv, col[:, None] * inv], -1)
    cos = jnp.repeat(jnp.cos(ang), 2, axis=1)
    sin = jnp.repeat(jnp.sin(ang), 2, axis=1)
    sign = jnp.tile(jnp.asarray([-1.0, 1.0], f32), HEAD_DIM // 2)
    return jnp.tile(cos, (1, 2)), jnp.tile(sin * sign, (1, 2))


def _seg_matrix():
    r = np.arange(LANES)
    return jnp.asarray((r[:, None] // HEAD_DIM == r[None, :] // HEAD_DIM) / HEAD_DIM, bf16)


def _layer(x, batch, p, cos_t, sin_t, bd):
    n = x.shape[0]
    cap = 2 * n // N_EXPERTS
    hna, qn, kn, vb, hsg = _proj(x, p["w_in"], cos_t, sin_t, p["q_norm"], p["k_norm"], bd, batch)
    ya = _na(hna.reshape(batch, SEQ, -1), p["na_bias"]).reshape(n, -1)
    yb = _gqa(qn.reshape(batch, SEQ, -1), kn, vb).reshape(n, -1)
    yc = _sgu(hsg, p["sg_w"], p["sg_b"], p["sg_ln_g"], p["sg_ln_b"], bd)
    x1, aff = _outproj(ya, yb, yc, x, p["w_out"], p["ln1_g"], p["ln1_b"], p["wr_hi"], p["wr_lo"])
    aff_t = aff.T.reshape(N_EXPERTS, n // LANES, LANES)
    sel = _route1(aff_t, cap)
    idx, gate, rho, off, cnt = _route2(aff_t, sel, cap)
    z = _ffn(idx.reshape(N_EXPERTS, cap), rho.reshape(N_EXPERTS, cap), x1, gate,
             p["w_gate"], p["w_up"], p["w_down"], cap)
    row_starts = jnp.concatenate(
        [off.reshape(-1)[::COMBINE_TT], jnp.full((1,), z.shape[0], f32)]).astype(i32)
    return _combine(row_starts, x1, off, cnt, p["ln2_g"], p["ln2_b"], z)


def _trunk(x, ln_in_g, ln_in_b, layers, cos_t, sin_t, bd):
    batch = x.shape[0]
    h = _ln_in(x.reshape(batch * SEQ, D_MODEL), ln_in_g, ln_in_b)
    for p in layers:
        h = _layer(h, batch, p, cos_t, sin_t, bd)
    return h.reshape(batch, SEQ, D_MODEL)


def _layer_params(l, w_in, na_rpb, q_norm, k_norm, sg_w, sg_b, sg_ln_g, sg_ln_b, w_out, ln1_g, ln1_b,
                  w_router, w_gate, w_up, w_down, ln2_g, ln2_b):
    wr = w_router[l]
    wr_hi = wr.astype(bf16)
    width = SG_GROUPS * HEAD_DIM
    return {
        "w_in": w_in[l].astype(bf16),
        "na_bias": _na_bias_table(na_rpb[l]),
        "q_norm": jnp.tile(q_norm[l], 2).reshape(1, LANES),
        "k_norm": jnp.tile(k_norm[l], 2).reshape(1, LANES),
        "sg_w": sg_w[l].astype(bf16),
        "sg_b": jnp.repeat(sg_b[l].T, HEAD_DIM, axis=1),
        "sg_ln_g": sg_ln_g[l].reshape(1, width),
        "sg_ln_b": sg_ln_b[l].reshape(1, width),
        "w_out": w_out[l].astype(bf16),
        "ln1_g": ln1_g[l].reshape(1, -1), "ln1_b": ln1_b[l].reshape(1, -1),
        "wr_hi": wr_hi, "wr_lo": (wr - wr_hi.astype(f32)).astype(bf16),
        "w_gate": w_gate[l].astype(bf16), "w_up": w_up[l].astype(bf16), "w_down": w_down[l].astype(bf16),
        "ln2_g": ln2_g[l].reshape(1, -1), "ln2_b": ln2_b[l].reshape(1, -1),
    }


def kernel(x_prompt, x_sample, ln_in_g, ln_in_b, w_in, na_rpb, q_norm, k_norm, sg_w, sg_b, sg_ln_g, sg_ln_b,
           w_out, ln1_g, ln1_b, w_router, w_gate, w_up, w_down, ln2_g, ln2_b):
    layers = [_layer_params(l, w_in, na_rpb, q_norm, k_norm, sg_w, sg_b, sg_ln_g, sg_ln_b, w_out, ln1_g,
                            ln1_b, w_router, w_gate, w_up, w_down, ln2_g, ln2_b)
              for l in range(w_in.shape[0])]
    cos_t, sin_t = _rope_tables()
    bd = _seg_matrix()
    return (_trunk(x_prompt, ln_in_g, ln_in_b, layers, cos_t, sin_t, bd),
            _trunk(x_sample, ln_in_g, ln_in_b, layers, cos_t, sin_t, bd))
```

```python
import functools

import numpy as np
import jax
import jax.numpy as jnp
from jax import lax
from jax.experimental import pallas as pl
from jax.experimental.pallas import tpu as pltpu

f32 = jnp.float32
bf16 = jnp.bfloat16
i32 = jnp.int32

D_MODEL = 1024
SEQ = 4096
DEPTH = 4
GRID_W = 64
GRID_ROWS = SEQ // GRID_W
HEAD_DIM = 64
NA_HEADS = 4
NA_KH = 8
NA_KW = 16
GQA_HEADS = 8
GQA_KV = 2
ROPE_THETA = 10000.0
SG_GROUPS = 4
SG_CHUNK = 128
N_EXPERTS = 16
D_EXPERT = 2048
LN_EPS = 1e-5
QK_EPS = 1e-6
DN_ALPHA = (2 * DEPTH) ** 0.25
QK_SCALE = HEAD_DIM ** -0.5
LOG2E = 1.4426950408889634

LANES = 128
NEG = -1e30
MIB = 1024 * 1024

NA_ROWS_PER_STEP = 4
NA_WIN = 12
GQA_TQ = 128
GQA_TK = 512
ROW_TILE = 512
FFN_TM = 256
FFN_FCHUNK = 512
DMA_UNROLL = 8
ROUTE_PW = 512
COMBINE_TT = 256
COMBINE_RC = 256
SMEM_INDEX_BYTES = 128 * 1024


def _cp(semantics, vmem_mib=48):
    return pltpu.CompilerParams(dimension_semantics=semantics, vmem_limit_bytes=vmem_mib * MIB)


def _dot(a, b):
    return jnp.dot(a, b, preferred_element_type=f32)


def _dot_nt(a, b):
    return lax.dot_general(a, b, (((1,), (1,)), ((), ())), preferred_element_type=f32)


def _layer_norm(x, g, b):
    mu = jnp.mean(x, -1, keepdims=True)
    xc = x - mu
    var = jnp.mean(xc * xc, -1, keepdims=True)
    return xc * lax.rsqrt(var + LN_EPS) * g + b


def _split2(x):
    hi = x.astype(bf16)
    lo = (x - hi.astype(f32)).astype(bf16)
    return hi, lo


SLAB = D_MODEL // LANES


def _load_slabs(ref, rows):
    return jnp.concatenate([ref[pl.ds(a, rows, stride=SLAB), :] for a in range(SLAB)], axis=1)


def _store_slabs(ref, x):
    rows = x.shape[0]
    for a in range(SLAB):
        ref[pl.ds(a, rows, stride=SLAB), :] = x[:, a * LANES:(a + 1) * LANES]


def _seg_mean(x, bd):
    hi, lo = _split2(x)
    return _dot(hi, bd) + _dot(lo, bd)


def _ln_in_kernel(x_ref, g_ref, b_ref, o_ref):
    o_ref[...] = _layer_norm(x_ref[...], g_ref[...], b_ref[...])


def _ln_in(x, g, b):
    n = x.shape[0]
    tm = 1024
    row = pl.BlockSpec((tm, D_MODEL), lambda i: (i, 0))
    vec = pl.BlockSpec((1, D_MODEL), lambda i: (0, 0))
    return pl.pallas_call(
        _ln_in_kernel, name="ln_in", grid=(n // tm,), in_specs=[row, vec, vec], out_specs=row,
        out_shape=jax.ShapeDtypeStruct((n, D_MODEL), f32), compiler_params=_cp(("parallel",)),
    )(x, g.reshape(1, -1), b.reshape(1, -1))


def _proj_kernel(x_ref, w_ref, cos_ref, sin_ref, qg_ref, kg_ref, bd_ref,
                 hna_ref, qn_ref, kn_ref, vb_ref, hsg_ref):
    tm = x_ref.shape[0]
    xb = x_ref[...].astype(bf16)

    def mm(c):
        return _dot(xb, w_ref[:, 256 * c:256 * (c + 1)])

    hna_ref[:, 0:256] = (mm(0) * QK_SCALE).astype(bf16)
    hna_ref[:, 256:512] = mm(1).astype(bf16)
    hna_ref[:, 512:768] = mm(2).astype(bf16)

    cos = cos_ref[...]
    sin = sin_ref[...]
    bd = bd_ref[...]
    even = (lax.broadcasted_iota(i32, (tm, LANES), 1) & 1) == 0

    def norm_rope(x, g):
        y = x * lax.rsqrt(_seg_mean(x * x, bd) + QK_EPS) * g
        partner = jnp.where(even, pltpu.roll(y, LANES - 1, 1), pltpu.roll(y, 1, 1))
        return y * cos + partner * sin

    qg = qg_ref[...]
    for c in (3, 4):
        a = mm(c)
        for j in range(2):
            col = (c - 3) * 256 + j * LANES
            qn_ref[:, col:col + LANES] = (
                norm_rope(a[:, j * LANES:(j + 1) * LANES], qg) * (QK_SCALE * LOG2E)).astype(bf16)
    a = mm(5)
    kn = norm_rope(a[:, :LANES], kg_ref[...]).astype(bf16)
    kn_ref[0, 0] = kn[:, :HEAD_DIM]
    kn_ref[0, 1] = kn[:, HEAD_DIM:]
    vb = a[:, LANES:].astype(bf16)
    vb_ref[0, 0] = vb[:, :HEAD_DIM]
    vb_ref[0, 1] = vb[:, HEAD_DIM:]
    hsg_ref[:, :256] = mm(6)
    hsg_ref[:, 256:] = mm(7)


def _proj(x, w, cos_t, sin_t, qg, kg, bd, batch):
    n = x.shape[0]
    tm = ROW_TILE
    per_seq = SEQ // tm
    row = lambda width: pl.BlockSpec((tm, width), lambda i: (i, 0))
    const = lambda shape: pl.BlockSpec(shape, lambda i: tuple(0 for _ in shape))
    tab = pl.BlockSpec((tm, LANES), lambda i: (i % per_seq, 0))
    head_major = pl.BlockSpec((1, GQA_KV, tm, HEAD_DIM), lambda i: (i // per_seq, 0, i % per_seq, 0))
    return pl.pallas_call(
        _proj_kernel, name="proj", grid=(n // tm,),
        in_specs=[row(D_MODEL), const(w.shape), tab, tab, const((1, LANES)), const((1, LANES)),
                  const((LANES, LANES))],
        out_specs=[row(768), row(512), head_major, head_major, row(512)],
        out_shape=[jax.ShapeDtypeStruct((n, 768), bf16),
                   jax.ShapeDtypeStruct((n, 512), bf16),
                   jax.ShapeDtypeStruct((batch, GQA_KV, SEQ, HEAD_DIM), bf16),
                   jax.ShapeDtypeStruct((batch, GQA_KV, SEQ, HEAD_DIM), bf16),
                   jax.ShapeDtypeStruct((n, 512), f32)],
        compiler_params=_cp(("parallel",)),
    )(x, w, cos_t, sin_t, qg, kg, bd)


def _na_window_start(r0):
    return jnp.clip(r0 - NA_KH // 2, 0, GRID_ROWS - NA_WIN)


def _na_kernel(q_ref, k_ref, v_ref, bias_ref, o_ref):
    ws = _na_window_start(pl.program_id(1) * NA_ROWS_PER_STEP)
    krow = pl.multiple_of(ws * GRID_W, GRID_W)
    q = q_ref[0]
    kw = k_ref[0, pl.ds(krow, NA_WIN * GRID_W), :]
    vw = v_ref[0, pl.ds(krow, NA_WIN * GRID_W), :]
    outs = []
    for h in range(NA_HEADS):
        cs = slice(h * HEAD_DIM, (h + 1) * HEAD_DIM)
        s = _dot_nt(q[:, cs], kw[:, cs]) + bias_ref[0, h]
        m = jnp.max(s, -1, keepdims=True)
        e = jnp.exp(s - m)
        l = jnp.sum(e, -1, keepdims=True)
        outs.append(_dot(e.astype(bf16), vw[:, cs]) / l)
    o_ref[0] = jnp.concatenate(outs, -1).astype(bf16)


def _na_pattern(i):
    last = GRID_ROWS // NA_ROWS_PER_STEP - 1
    return jnp.where(i == 0, 0, jnp.where(i == last, 2, 1))


def _na(hna, bias_tab):
    batch = hna.shape[0]
    tq = NA_ROWS_PER_STEP * GRID_W
    width = NA_HEADS * HEAD_DIM
    return pl.pallas_call(
        _na_kernel, name="na", grid=(batch, SEQ // tq),
        in_specs=[pl.BlockSpec((1, tq, width), lambda b, i: (b, i, 0)),
                  pl.BlockSpec((1, SEQ, width), lambda b, i: (b, 0, 1)),
                  pl.BlockSpec((1, SEQ, width), lambda b, i: (b, 0, 2)),
                  pl.BlockSpec((1,) + bias_tab.shape[1:], lambda b, i: (_na_pattern(i), 0, 0, 0))],
        out_specs=pl.BlockSpec((1, tq, width), lambda b, i: (b, i, 0)),
        out_shape=jax.ShapeDtypeStruct((batch, SEQ, width), bf16),
        compiler_params=_cp(("parallel", "arbitrary")),
    )(hna, hna, hna, bias_tab)


def _na_bias_table(rpb):
    col = np.arange(GRID_W)
    col_start = np.clip(col - NA_KW // 2, 0, GRID_W - NA_KW)
    kc = np.arange(GRID_W)
    col_ok = (kc[None, :] >= col_start[:, None]) & (kc[None, :] < col_start[:, None] + NA_KW)
    dcol = np.clip(kc[None, :] - col[:, None] + NA_KW - 1, 0, 2 * NA_KW - 2)
    tabs = []
    for r0 in (0, NA_ROWS_PER_STEP, GRID_ROWS - NA_ROWS_PER_STEP):
        ws = int(np.clip(r0 - NA_KH // 2, 0, GRID_ROWS - NA_WIN))
        r = r0 + np.arange(NA_ROWS_PER_STEP)
        rs = np.clip(r - NA_KH // 2, 0, GRID_ROWS - NA_KH)
        krow = ws + np.arange(NA_WIN)
        row_ok = (krow[None, :] >= rs[:, None]) & (krow[None, :] < rs[:, None] + NA_KH)
        drow = np.clip(krow[None, :] - r[:, None] + NA_KH - 1, 0, 2 * NA_KH - 2)
        g = rpb[:, drow][:, :, :, dcol]
        ok = row_ok[:, :, None, None] & col_ok[None, None]
        g = jnp.where(ok[None], g, NEG)
        tabs.append(jnp.transpose(g, (0, 1, 3, 2, 4)).reshape(
            NA_HEADS, NA_ROWS_PER_STEP * GRID_W, NA_WIN * GRID_W))
    return jnp.stack(tabs).astype(f32)


def _gqa_kernel(q_ref, k_ref, v_ref, o_ref):
    g_per_kv = GQA_HEADS // GQA_KV
    q = q_ref[0]
    q4 = jnp.concatenate([q[:, g * HEAD_DIM:(g + 1) * HEAD_DIM] for g in range(g_per_kv)], axis=0)
    rows = g_per_kv * GQA_TQ

    m = jnp.full((rows, 1), NEG, f32)
    l = jnp.zeros((rows, 1), f32)
    acc = jnp.zeros((rows, HEAD_DIM), f32)
    for j in range(SEQ // GQA_TK):
        kc = k_ref[0, 0, j * GQA_TK:(j + 1) * GQA_TK, :]
        vc = v_ref[0, 0, j * GQA_TK:(j + 1) * GQA_TK, :]
        s = _dot_nt(q4, kc)
        mn = jnp.maximum(m, jnp.max(s, -1, keepdims=True))
        alpha = jnp.exp2(m - mn)
        p = jnp.exp2(s - mn)
        l = alpha * l + jnp.sum(p, -1, keepdims=True)
        acc = alpha * acc + _dot(p.astype(bf16), vc)
        m = mn
    o = acc / l
    for g in range(g_per_kv):
        o_ref[0, :, g * HEAD_DIM:(g + 1) * HEAD_DIM] = o[g * GQA_TQ:(g + 1) * GQA_TQ].astype(bf16)


def _gqa(qn, kn, vb):
    batch = qn.shape[0]
    width = (GQA_HEADS // GQA_KV) * HEAD_DIM
    qspec = pl.BlockSpec((1, GQA_TQ, width), lambda b, kv, i: (b, i, kv))
    kvspec = pl.BlockSpec((1, 1, SEQ, HEAD_DIM), lambda b, kv, i: (b, kv, 0, 0))
    return pl.pallas_call(
        _gqa_kernel, name="gqa", grid=(batch, GQA_KV, SEQ // GQA_TQ),
        in_specs=[qspec, kvspec, kvspec], out_specs=qspec,
        out_shape=jax.ShapeDtypeStruct((batch, SEQ, GQA_HEADS * HEAD_DIM), bf16),
        compiler_params=_cp(("parallel", "parallel", "arbitrary")),
    )(qn, kn, vb)


def _sgu_kernel(h_ref, ws_ref, bs_ref, g_ref, b_ref, bd_ref, o_ref):
    tm = h_ref.shape[0]
    width = SG_GROUPS * HEAD_DIM
    bd = bd_ref[...]
    u = jax.nn.gelu(h_ref[:, :width])
    v = jax.nn.gelu(h_ref[:, width:])
    parts = []
    for j in range(width // LANES):
        x = v[:, j * LANES:(j + 1) * LANES]
        xc = x - _seg_mean(x, bd)
        parts.append(xc * lax.rsqrt(_seg_mean(xc * xc, bd) + LN_EPS))
    vn = (jnp.concatenate(parts, 1) * g_ref[...] + b_ref[...]).astype(bf16)
    bs = bs_ref[...]
    for c in range(tm // SG_CHUNK):
        rows = slice(c * SG_CHUNK, (c + 1) * SG_CHUNK)
        mixed = jnp.concatenate(
            [_dot(ws_ref[g], vn[rows, g * HEAD_DIM:(g + 1) * HEAD_DIM]) for g in range(SG_GROUPS)], axis=1)
        o_ref[rows, :] = (u[rows] * (mixed + bs)).astype(bf16)


def _sgu(hsg, ws, bs_tab, g, b, bd):
    n = hsg.shape[0]
    tm = ROW_TILE
    width = SG_GROUPS * HEAD_DIM
    const = lambda shape: pl.BlockSpec(shape, lambda i: tuple(0 for _ in shape))
    return pl.pallas_call(
        _sgu_kernel, name="sgu", grid=(n // tm,),
        in_specs=[pl.BlockSpec((tm, 2 * width), lambda i: (i, 0)), const(ws.shape), const(bs_tab.shape),
                  const((1, width)), const((1, width)), const((LANES, LANES))],
        out_specs=pl.BlockSpec((tm, width), lambda i: (i, 0)),
        out_shape=jax.ShapeDtypeStruct((n, width), bf16),
        compiler_params=_cp(("parallel",)),
    )(hsg, ws, bs_tab, g, b, bd)


def _outproj_kernel(ya_ref, yb_ref, yc_ref, x_ref, w_ref, g_ref, b_ref, wrh_ref, wrl_ref, x1_ref, aff_ref):
    mix = (_dot(ya_ref[...], w_ref[0:256]) + _dot(yb_ref[...], w_ref[256:768])
           + _dot(yc_ref[...], w_ref[768:1024]))
    x1 = _layer_norm(DN_ALPHA * x_ref[...] + mix, g_ref[...], b_ref[...])
    _store_slabs(x1_ref, x1)
    hi, lo = _split2(x1)
    wrh = wrh_ref[...]
    logits = _dot(hi, wrh) + _dot(lo, wrh) + _dot(hi, wrl_ref[...])
    m = jnp.max(logits, -1, keepdims=True)
    e = jnp.exp(logits - m)
    aff_ref[...] = e / jnp.sum(e, -1, keepdims=True)


def _outproj(ya, yb, yc, x, w, g, b, wrh, wrl):
    n = x.shape[0]
    tm = ROW_TILE
    row = lambda width: pl.BlockSpec((tm, width), lambda i: (i, 0))
    const = lambda shape: pl.BlockSpec(shape, lambda i: tuple(0 for _ in shape))
    return pl.pallas_call(
        _outproj_kernel, name="outproj", grid=(n // tm,),
        in_specs=[row(256), row(512), row(256), row(D_MODEL), const(w.shape), const((1, D_MODEL)),
                  const((1, D_MODEL)), const(wrh.shape), const(wrl.shape)],
        out_specs=[pl.BlockSpec((tm * SLAB, LANES), lambda i: (i, 0)), row(N_EXPERTS)],
        out_shape=[jax.ShapeDtypeStruct((n * SLAB, LANES), f32), jax.ShapeDtypeStruct((n, N_EXPERTS), f32)],
        compiler_params=_cp(("parallel",)),
    )(ya, yb, yc, x, w, g, b, wrh, wrl)


def _tri_incl():
    r = lax.broadcasted_iota(i32, (LANES, LANES), 0)
    c = lax.broadcasted_iota(i32, (LANES, LANES), 1)
    return (r <= c).astype(bf16)


def _tri_strict_lower(nc):
    r = lax.broadcasted_iota(i32, (nc, nc), 0)
    c = lax.broadcasted_iota(i32, (nc, nc), 1)
    return (c < r).astype(bf16)


def _byte_planes(x, planes):
    xi = x.astype(i32)
    return [((xi >> (8 * k)) & 255).astype(f32).astype(bf16) for k in range(planes)]


def _token_prefix(m, planes):
    nc = m.shape[0]
    lincl = _dot(m.astype(bf16), _tri_incl())
    tot = jnp.broadcast_to(lincl[:, LANES - 1:LANES], m.shape)
    sl = _tri_strict_lower(nc)
    pref = sum(_dot(sl, p) * float(256 ** k) for k, p in enumerate(_byte_planes(tot, planes)))
    return lincl, tot, pref


def _route1_kernel(aff_ref, sel_ref, *, cap):
    bits = lax.bitcast_convert_type(aff_ref[0], i32)

    def body(i, t):
        cand = t | lax.shift_left(jnp.int32(1), 30 - i)
        cnt = jnp.sum((bits >= cand).astype(f32))
        return jnp.where(cnt >= cap, cand, t)

    thr = lax.fori_loop(0, 31, body, jnp.int32(0))
    gt = bits > thr
    eq = bits == thr
    need = cap - jnp.sum(gt.astype(f32))
    eqf = eq.astype(f32)
    lincl, _, pref = _token_prefix(eqf, 1)
    before = pref + lincl - eqf
    sel_ref[0] = (gt | (eq & (before < need))).astype(f32)


def _route1(aff_t, cap):
    ne, nc, _ = aff_t.shape
    spec = pl.BlockSpec((1, nc, LANES), lambda e: (e, 0, 0))
    return pl.pallas_call(
        functools.partial(_route1_kernel, cap=cap), name="route1", grid=(ne,), in_specs=[spec],
        out_specs=spec,
        out_shape=jax.ShapeDtypeStruct(aff_t.shape, f32), compiler_params=_cp(("parallel",)),
    )(aff_t)


def _route2_kernel(aff_ref, sel_ref, idx_ref, gate_ref, rho_ref, off_ref, cnt_ref, *, cap):
    e = pl.program_id(0)
    nc = aff_ref.shape[1]
    pw = ROUTE_PW

    count = jnp.zeros((nc, LANES), f32)
    rank = jnp.zeros((nc, LANES), f32)
    for k in range(N_EXPERTS):
        sk = sel_ref[k]
        count = count + sk
        rank = rank + jnp.where(k < e, sk, 0.0)
    lc, _, pc = _token_prefix(count, 2)
    off = pc + lc - count
    off_ref[...] = off
    cnt_ref[...] = count

    sel = sel_ref[e]
    lincl, tot, pref = _token_prefix(sel, 1)
    cse = pref[:, 0:1]
    csi = cse + tot[:, 0:1]
    chunk_id = lax.broadcasted_iota(i32, (nc, 1), 0).astype(f32)

    lt = lincl.T.astype(bf16)
    a = aff_ref[0].T
    a1 = a.astype(bf16)
    r1 = a - a1.astype(f32)
    a2 = r1.astype(bf16)
    a3 = (r1 - a2.astype(f32)).astype(bf16)
    dest = _byte_planes((off + rank).T, 3)
    lane_id = lax.broadcasted_iota(i32, (LANES, pw), 0).astype(f32)

    def tile(k, carry):
        p = (k * pw + lax.broadcasted_iota(i32, (1, pw), 1)).astype(f32)
        oh = (cse <= p) & (p < csi)
        ohb = oh.astype(f32).astype(bf16)
        chunk = jnp.sum(jnp.where(oh, chunk_id, 0.0), axis=0, keepdims=True)
        local = p - jnp.sum(jnp.where(oh, cse, 0.0), axis=0, keepdims=True)
        jstar = jnp.sum((_dot(lt, ohb) <= local).astype(f32), axis=0, keepdims=True)
        hit = lane_id == jstar
        g = _dot(a1, ohb) + _dot(a2, ohb) + _dot(a3, ohb)
        r = _dot(dest[0], ohb) + 256.0 * _dot(dest[1], ohb) + 65536.0 * _dot(dest[2], ohb)
        idx_ref[0, pl.ds(k, 1), :] = (chunk * LANES + jstar).astype(i32)
        gate_ref[0, pl.ds(k, 1), :] = jnp.sum(jnp.where(hit, g, 0.0), axis=0, keepdims=True)
        rho_ref[0, pl.ds(k, 1), :] = jnp.sum(jnp.where(hit, r, 0.0), axis=0, keepdims=True).astype(i32)
        return carry

    lax.fori_loop(0, cap // pw, tile, 0)


def _route2(aff_t, sel, cap):
    ne, nc, _ = aff_t.shape
    pw = ROUTE_PW
    slot = pl.BlockSpec((1, cap // pw, pw), lambda e: (e, 0, 0))
    tok = pl.BlockSpec((nc, LANES), lambda e: (0, 0))
    slot_shape = lambda dt: jax.ShapeDtypeStruct((ne, cap // pw, pw), dt)
    return pl.pallas_call(
        functools.partial(_route2_kernel, cap=cap), name="route2", grid=(ne,),
        in_specs=[pl.BlockSpec((1, nc, LANES), lambda e: (e, 0, 0)),
                  pl.BlockSpec((ne, nc, LANES), lambda e: (0, 0, 0))],
        out_specs=[slot, slot, slot, tok, tok],
        out_shape=[slot_shape(i32), slot_shape(f32), slot_shape(i32),
                   jax.ShapeDtypeStruct((nc, LANES), f32), jax.ShapeDtypeStruct((nc, LANES), f32)],
        compiler_params=_cp(("arbitrary",)),
    )(aff_t, sel)


def _row_to_col(row):
    n = row.shape[1]
    r = lax.broadcasted_iota(i32, (n, n), 0)
    c = lax.broadcasted_iota(i32, (n, n), 1)
    return jnp.sum(jnp.where(r == c, row, 0.0), axis=1, keepdims=True)


def _ffn_kernel(idx_sm, rho_sm, x_hbm, gate_ref, wg_ref, wu_ref, wd_ref, *rest, aliased):
    if aliased:
        _, z_hbm, xbuf, ybuf, gsem, ssem = rest
    else:
        z_hbm, xbuf, ybuf, gsem, ssem = rest
    tm = FFN_TM
    mt = pl.num_programs(1)
    total = pl.num_programs(0) * mt
    s = pl.program_id(0) * mt + pl.program_id(1)
    slot = s % 2

    def gather_copy(tok, i, sl):
        return pltpu.make_async_copy(x_hbm.at[pl.ds(pl.multiple_of(tok * SLAB, SLAB), SLAB), :],
                                     xbuf.at[sl, pl.ds(pl.multiple_of(i * SLAB, SLAB), SLAB), :], gsem.at[sl])

    def scatter_copy(row, i, sl):
        return pltpu.make_async_copy(ybuf.at[sl, pl.ds(pl.multiple_of(i * SLAB, SLAB), SLAB), :],
                                     z_hbm.at[pl.ds(pl.multiple_of(row * SLAB, SLAB), SLAB), :], ssem.at[sl])

    def start_gather(step, sl):
        def body(i, c):
            gather_copy(idx_sm[step * tm + i], i, sl).start()
            return c
        lax.fori_loop(0, tm, body, 0, unroll=DMA_UNROLL)

    def wait_gather(sl):
        pltpu.make_async_copy(x_hbm.at[pl.ds(0, tm * SLAB), :], xbuf.at[sl], gsem.at[sl]).wait()

    def wait_scatter(sl):
        pltpu.make_async_copy(ybuf.at[sl], z_hbm.at[pl.ds(0, tm * SLAB), :], ssem.at[sl]).wait()

    @pl.when(s == 0)
    def _():
        start_gather(0, 0)

    @pl.when(s + 1 < total)
    def _():
        start_gather(s + 1, 1 - slot)

    wait_gather(slot)

    @pl.when(s >= 2)
    def _():
        wait_scatter(slot)

    x = _load_slabs(xbuf.at[slot], tm).astype(bf16)
    y = jnp.zeros((tm, D_MODEL), f32)
    for c in range(D_EXPERT // FFN_FCHUNK):
        cols = slice(c * FFN_FCHUNK, (c + 1) * FFN_FCHUNK)
        hg = _dot(x, wg_ref[0, :, cols])
        hu = _dot(x, wu_ref[0, :, cols])
        y = y + _dot((jax.nn.silu(hg) * hu).astype(bf16), wd_ref[0, cols, :])
    _store_slabs(ybuf.at[slot], y * _row_to_col(gate_ref[0]))

    def start_scatter(i, c):
        scatter_copy(rho_sm[s * tm + i], i, slot).start()
        return c
    lax.fori_loop(0, tm, start_scatter, 0, unroll=DMA_UNROLL)

    @pl.when(s == total - 1)
    def _():
        wait_scatter(slot)

        @pl.when(total >= 2)
        def _():
            wait_scatter(1 - slot)


def _ffn_call(idx, rho, x1, gate, wg, wu, wd, z, e0, eg, cap):
    tm = FFN_TM
    mt = cap // tm
    n_rows = N_EXPERTS * cap
    aliased = z is not None
    any_spec = pl.BlockSpec(memory_space=pl.ANY)
    in_specs = [any_spec,
                pl.BlockSpec((1, 1, tm), lambda e, m, *_: ((e0 + e) * mt + m, 0, 0)),
                pl.BlockSpec((1, D_MODEL, D_EXPERT), lambda e, m, *_: (e0 + e, 0, 0)),
                pl.BlockSpec((1, D_MODEL, D_EXPERT), lambda e, m, *_: (e0 + e, 0, 0)),
                pl.BlockSpec((1, D_EXPERT, D_MODEL), lambda e, m, *_: (e0 + e, 0, 0))]
    args = [x1, gate, wg, wu, wd]
    if aliased:
        in_specs.append(any_spec)
        args.append(z)
    grid_spec = pltpu.PrefetchScalarGridSpec(
        num_scalar_prefetch=2, grid=(eg, mt), in_specs=in_specs, out_specs=any_spec,
        scratch_shapes=[pltpu.VMEM((2, tm * SLAB, LANES), f32), pltpu.VMEM((2, tm * SLAB, LANES), f32),
                        pltpu.SemaphoreType.DMA((2,)), pltpu.SemaphoreType.DMA((2,))])
    return pl.pallas_call(
        functools.partial(_ffn_kernel, aliased=aliased), name="ffn", grid_spec=grid_spec,
        out_shape=jax.ShapeDtypeStruct((n_rows * SLAB, LANES), f32),
        input_output_aliases={7: 0} if aliased else {},
        compiler_params=pltpu.CompilerParams(dimension_semantics=("arbitrary", "arbitrary"),
                                             vmem_limit_bytes=56 * MIB, has_side_effects=True),
    )(idx[e0:e0 + eg].reshape(-1), rho[e0:e0 + eg].reshape(-1), *args)


def _ffn(idx, rho, x1, gate, wg, wu, wd, cap):
    eg = max(1, min(N_EXPERTS, SMEM_INDEX_BYTES // (2 * 4 * cap)))
    gate3 = gate.reshape(N_EXPERTS * cap // FFN_TM, 1, FFN_TM)
    z = None
    for e0 in range(0, N_EXPERTS, eg):
        z = _ffn_call(idx, rho, x1, gate3, wg, wu, wd, z, e0, eg, cap)
    return z


def _combine_kernel(rs_sm, x_ref, off_ref, cnt_ref, g_ref, b_ref, z_hbm, o_ref, zbuf, sem, *, total_rows):
    tt = COMBINE_TT
    rc = COMBINE_RC
    i = pl.program_id(0)

    def tile_rows(t):
        first = (rs_sm[t] >> 3) << 3
        return first, (rs_sm[t + 1] - first + rc - 1) >> (rc.bit_length() - 1)

    r0, nch = tile_rows(i)

    offs = []
    ends = []
    for c in range(tt // LANES):
        o_row = off_ref[0, c:c + 1, :]
        offs.append(_row_to_col(o_row))
        ends.append(_row_to_col(o_row + cnt_ref[0, c:c + 1, :]))
    off_col = jnp.concatenate(offs, 0)
    end_col = jnp.concatenate(ends, 0)

    def chunk_start(first, j):
        return pl.multiple_of(jnp.minimum(first + j * rc, total_rows - rc), 8)

    def chunk_copy(first, j, sl):
        return pltpu.make_async_copy(z_hbm.at[pl.ds(chunk_start(first, j) * SLAB, rc * SLAB), :], zbuf.at[sl],
                                     sem.at[sl])

    @pl.when((i == 0) & (nch > 0))
    def _():
        chunk_copy(r0, 0, 0).start()

    def body(j, acc):
        sl = j % 2

        @pl.when(j + 1 < nch)
        def _():
            chunk_copy(r0, j + 1, 1 - sl).start()

        chunk_copy(r0, j, sl).wait()
        rowid = chunk_start(r0, j) + lax.broadcasted_iota(i32, (1, rc), 1)
        fresh = rowid >= r0 + j * rc
        rowf = rowid.astype(f32)
        seg = ((off_col <= rowf) & (rowf < end_col) & fresh).astype(f32).astype(bf16)
        hi, lo = _split2(_load_slabs(zbuf.at[sl], rc))
        return acc + _dot(seg, hi) + _dot(seg, lo)

    acc = lax.fori_loop(0, nch, body, jnp.zeros((tt, D_MODEL), f32))

    @pl.when(i + 1 < pl.num_programs(0))
    def _():
        nxt, nch_nxt = tile_rows(i + 1)

        @pl.when(nch_nxt > 0)
        def _():
            chunk_copy(nxt, 0, 0).start()

    o_ref[...] = _layer_norm(DN_ALPHA * _load_slabs(x_ref, tt) + acc, g_ref[...], b_ref[...])


def _combine(row_starts, x1, off, cnt, g, b, z):
    n = x1.shape[0] // SLAB
    tt = COMBINE_TT
    per = tt // LANES
    tok = pl.BlockSpec((1, per, LANES), lambda i, *_: (i, 0, 0))
    vec = pl.BlockSpec((1, D_MODEL), lambda i, *_: (0, 0))
    row = pl.BlockSpec((tt, D_MODEL), lambda i, *_: (i, 0))
    grid_spec = pltpu.PrefetchScalarGridSpec(
        num_scalar_prefetch=1, grid=(n // tt,),
        in_specs=[pl.BlockSpec((tt * SLAB, LANES), lambda i, *_: (i, 0)), tok, tok, vec, vec,
                  pl.BlockSpec(memory_space=pl.ANY)], out_specs=row,
        scratch_shapes=[pltpu.VMEM((2, COMBINE_RC * SLAB, LANES), f32), pltpu.SemaphoreType.DMA((2,))])
    return pl.pallas_call(
        functools.partial(_combine_kernel, total_rows=z.shape[0] // SLAB), name="combine",
        grid_spec=grid_spec,
        out_shape=jax.ShapeDtypeStruct((n, D_MODEL), f32), compiler_params=_cp(("arbitrary",)),
    )(row_starts, x1, off.reshape(n // tt, per, LANES), cnt.reshape(n // tt, per, LANES), g, b, z)


def _rope_tables():
    t = np.arange(SEQ)
    row = (t // GRID_W).astype(np.float32)
    col = (t % GRID_W).astype(np.float32)
    n_freq = HEAD_DIM // 4
    inv = jnp.asarray(ROPE_THETA, f32) ** (-jnp.arange(n_freq, dtype=f32) / n_freq)
    ang = jnp.concatenate([row[:, None] * inv, col[:, None] * inv], -1)
    cos = jnp.repeat(jnp.cos(ang), 2, axis=1)
    sin = jnp.repeat(jnp.sin(ang), 2, axis=1)
    sign = jnp.tile(jnp.asarray([-1.0, 1.0], f32), HEAD_DIM // 2)
    return jnp.tile(cos, (1, 2)), jnp.tile(sin * sign, (1, 2))


def _seg_matrix():
    r = np.arange(LANES)
    return jnp.asarray((r[:, None] // HEAD_DIM == r[None, :] // HEAD_DIM) / HEAD_DIM, bf16)


def _layer(x, batch, p, cos_t, sin_t, bd):
    n = x.shape[0]
    cap = 2 * n // N_EXPERTS
    hna, qn, kn, vb, hsg = _proj(x, p["w_in"], cos_t, sin_t, p["q_norm"], p["k_norm"], bd, batch)
    ya = _na(hna.reshape(batch, SEQ, -1), p["na_bias"]).reshape(n, -1)
    yb = _gqa(qn.reshape(batch, SEQ, -1), kn, vb).reshape(n, -1)
    yc = _sgu(hsg, p["sg_w"], p["sg_b"], p["sg_ln_g"], p["sg_ln_b"], bd)
    x1, aff = _outproj(ya, yb, yc, x, p["w_out"], p["ln1_g"], p["ln1_b"], p["wr_hi"], p["wr_lo"])
    aff_t = aff.T.reshape(N_EXPERTS, n // LANES, LANES)
    sel = _route1(aff_t, cap)
    idx, gate, rho, off, cnt = _route2(aff_t, sel, cap)
    z = _ffn(idx.reshape(N_EXPERTS, cap), rho.reshape(N_EXPERTS, cap), x1, gate,
             p["w_gate"], p["w_up"], p["w_down"], cap)
    row_starts = jnp.concatenate(
        [off.reshape(-1)[::COMBINE_TT], jnp.full((1,), z.shape[0] // SLAB, f32)]).astype(i32)
    return _combine(row_starts, x1, off, cnt, p["ln2_g"], p["ln2_b"], z)


def _trunk(x, ln_in_g, ln_in_b, layers, cos_t, sin_t, bd):
    batch = x.shape[0]
    h = _ln_in(x.reshape(batch * SEQ, D_MODEL), ln_in_g, ln_in_b)
    for p in layers:
        h = _layer(h, batch, p, cos_t, sin_t, bd)
    return h.reshape(batch, SEQ, D_MODEL)


def _layer_params(l, w_in, na_rpb, q_norm, k_norm, sg_w, sg_b, sg_ln_g, sg_ln_b, w_out, ln1_g, ln1_b,
                  w_router, w_gate, w_up, w_down, ln2_g, ln2_b):
    wr = w_router[l]
    wr_hi = wr.astype(bf16)
    width = SG_GROUPS * HEAD_DIM
    return {
        "w_in": w_in[l].astype(bf16),
        "na_bias": _na_bias_table(na_rpb[l]),
        "q_norm": jnp.tile(q_norm[l], 2).reshape(1, LANES),
        "k_norm": jnp.tile(k_norm[l], 2).reshape(1, LANES),
        "sg_w": sg_w[l].astype(bf16),
        "sg_b": jnp.repeat(sg_b[l].T, HEAD_DIM, axis=1),
        "sg_ln_g": sg_ln_g[l].reshape(1, width),
        "sg_ln_b": sg_ln_b[l].reshape(1, width),
        "w_out": w_out[l].astype(bf16),
        "ln1_g": ln1_g[l].reshape(1, -1), "ln1_b": ln1_b[l].reshape(1, -1),
        "wr_hi": wr_hi, "wr_lo": (wr - wr_hi.astype(f32)).astype(bf16),
        "w_gate": w_gate[l].astype(bf16), "w_up": w_up[l].astype(bf16), "w_down": w_down[l].astype(bf16),
        "ln2_g": ln2_g[l].reshape(1, -1), "ln2_b": ln2_b[l].reshape(1, -1),
    }


def kernel(x_prompt, x_sample, ln_in_g, ln_in_b, w_in, na_rpb, q_norm, k_norm, sg_w, sg_b, sg_ln_g, sg_ln_b,
           w_out, ln1_g, ln1_b, w_router, w_gate, w_up, w_down, ln2_g, ln2_b):
    layers = [_layer_params(l, w_in, na_rpb, q_norm, k_norm, sg_w, sg_b, sg_ln_g, sg_ln_b, w_out, ln1_g,
                            ln1_b, w_router, w_gate, w_up, w_down, ln2_g, ln2_b)
              for l in range(w_in.shape[0])]
    cos_t, sin_t = _rope_tables()
    bd = _seg_matrix()
    return (_trunk(x_prompt, ln_in_g, ln_in_b, layers, cos_t, sin_t, bd),
            _trunk(x_sample, ln_in_g, ln_in_b, layers, cos_t, sin_t, bd))
```

```python
import functools

import numpy as np
import jax
import jax.numpy as jnp
from jax import lax
from jax.experimental import pallas as pl
from jax.experimental.pallas import tpu as pltpu

f32 = jnp.float32
bf16 = jnp.bfloat16
i32 = jnp.int32

D_MODEL = 1024
SEQ = 4096
DEPTH = 4
GRID_W = 64
GRID_ROWS = SEQ // GRID_W
HEAD_DIM = 64
NA_HEADS = 4
NA_KH = 8
NA_KW = 16
GQA_HEADS = 8
GQA_KV = 2
ROPE_THETA = 10000.0
SG_GROUPS = 4
SG_CHUNK = 128
N_EXPERTS = 16
D_EXPERT = 2048
LN_EPS = 1e-5
QK_EPS = 1e-6
DN_ALPHA = (2 * DEPTH) ** 0.25
QK_SCALE = HEAD_DIM ** -0.5
LOG2E = 1.4426950408889634

LANES = 128
NEG = -1e30
MIB = 1024 * 1024

NA_ROWS_PER_STEP = 4
NA_WIN = 12
GQA_TQ = 256
GQA_TK = 512
ROW_TILE = 512
FFN_TM = 512
FFN_FCHUNK = 512
DMA_UNROLL = 8
ROUTE_PW = 512
COMBINE_TT = 256
COMBINE_RC = 256
SMEM_INDEX_BYTES = 128 * 1024


def _cp(semantics, vmem_mib=48):
    return pltpu.CompilerParams(dimension_semantics=semantics, vmem_limit_bytes=vmem_mib * MIB)


def _dot(a, b):
    return jnp.dot(a, b, preferred_element_type=f32)


def _dot_nt(a, b):
    return lax.dot_general(a, b, (((1,), (1,)), ((), ())), preferred_element_type=f32)


def _layer_norm(x, g, b):
    mu = jnp.mean(x, -1, keepdims=True)
    xc = x - mu
    var = jnp.mean(xc * xc, -1, keepdims=True)
    return xc * lax.rsqrt(var + LN_EPS) * g + b


def _split2(x):
    hi = x.astype(bf16)
    lo = (x - hi.astype(f32)).astype(bf16)
    return hi, lo


SLAB = D_MODEL // LANES


def _load_slabs(ref, rows):
    return jnp.concatenate([ref[pl.ds(a, rows, stride=SLAB), :] for a in range(SLAB)], axis=1)


def _store_slabs(ref, x):
    rows = x.shape[0]
    for a in range(SLAB):
        ref[pl.ds(a, rows, stride=SLAB), :] = x[:, a * LANES:(a + 1) * LANES]


def _seg_mean(x, bd):
    hi, lo = _split2(x)
    return _dot(hi, bd) + _dot(lo, bd)


def _ln_in_kernel(x_ref, g_ref, b_ref, o_ref):
    o_ref[...] = _layer_norm(x_ref[...], g_ref[...], b_ref[...])


def _ln_in(x, g, b):
    n = x.shape[0]
    tm = 1024
    row = pl.BlockSpec((tm, D_MODEL), lambda i: (i, 0))
    vec = pl.BlockSpec((1, D_MODEL), lambda i: (0, 0))
    return pl.pallas_call(
        _ln_in_kernel, name="ln_in", grid=(n // tm,), in_specs=[row, vec, vec], out_specs=row,
        out_shape=jax.ShapeDtypeStruct((n, D_MODEL), f32), compiler_params=_cp(("parallel",)),
    )(x, g.reshape(1, -1), b.reshape(1, -1))


def _proj_kernel(x_ref, w_ref, cos_ref, sin_ref, qg_ref, kg_ref, bd_ref,
                 hna_ref, qn_ref, kn_ref, vb_ref, hsg_ref):
    tm = x_ref.shape[0]
    xb = x_ref[...].astype(bf16)

    wide = {}

    def mm(c):
        if c // 2 not in wide:
            wide[c // 2] = _dot(xb, w_ref[:, 512 * (c // 2):512 * (c // 2 + 1)])
        return wide[c // 2][:, 256 * (c % 2):256 * (c % 2 + 1)]

    hna_ref[:, 0:256] = (mm(0) * QK_SCALE).astype(bf16)
    hna_ref[:, 256:512] = mm(1).astype(bf16)
    hna_ref[:, 512:768] = mm(2).astype(bf16)

    cos = cos_ref[...]
    sin = sin_ref[...]
    bd = bd_ref[...]
    even = (lax.broadcasted_iota(i32, (tm, LANES), 1) & 1) == 0

    def norm_rope(x, g):
        y = x * lax.rsqrt(_seg_mean(x * x, bd) + QK_EPS) * g
        partner = jnp.where(even, pltpu.roll(y, LANES - 1, 1), pltpu.roll(y, 1, 1))
        return y * cos + partner * sin

    qg = qg_ref[...]
    for c in (3, 4):
        a = mm(c)
        for j in range(2):
            col = (c - 3) * 256 + j * LANES
            qn_ref[:, col:col + LANES] = (
                norm_rope(a[:, j * LANES:(j + 1) * LANES], qg) * (QK_SCALE * LOG2E)).astype(bf16)
    a = mm(5)
    kn = norm_rope(a[:, :LANES], kg_ref[...]).astype(bf16)
    kn_ref[0, 0] = kn[:, :HEAD_DIM]
    kn_ref[0, 1] = kn[:, HEAD_DIM:]
    vb = a[:, LANES:].astype(bf16)
    ones = jnp.ones((tm, HEAD_DIM), bf16)
    vb_ref[0, 0] = jnp.concatenate([vb[:, :HEAD_DIM], ones], axis=1)
    vb_ref[0, 1] = jnp.concatenate([vb[:, HEAD_DIM:], ones], axis=1)
    hsg_ref[:, :256] = mm(6)
    hsg_ref[:, 256:] = mm(7)


def _proj(x, w, cos_t, sin_t, qg, kg, bd, batch):
    n = x.shape[0]
    tm = ROW_TILE
    per_seq = SEQ // tm
    row = lambda width: pl.BlockSpec((tm, width), lambda i: (i, 0))
    const = lambda shape: pl.BlockSpec(shape, lambda i: tuple(0 for _ in shape))
    tab = pl.BlockSpec((tm, LANES), lambda i: (i % per_seq, 0))
    head_major = lambda width: pl.BlockSpec((1, GQA_KV, tm, width),
                                            lambda i: (i // per_seq, 0, i % per_seq, 0))
    return pl.pallas_call(
        _proj_kernel, name="proj", grid=(n // tm,),
        in_specs=[row(D_MODEL), const(w.shape), tab, tab, const((1, LANES)), const((1, LANES)),
                  const((LANES, LANES))],
        out_specs=[row(768), row(512), head_major(HEAD_DIM), head_major(2 * HEAD_DIM), row(512)],
        out_shape=[jax.ShapeDtypeStruct((n, 768), bf16),
                   jax.ShapeDtypeStruct((n, 512), bf16),
                   jax.ShapeDtypeStruct((batch, GQA_KV, SEQ, HEAD_DIM), bf16),
                   jax.ShapeDtypeStruct((batch, GQA_KV, SEQ, 2 * HEAD_DIM), bf16),
                   jax.ShapeDtypeStruct((n, 512), f32)],
        compiler_params=_cp(("parallel",)),
    )(x, w, cos_t, sin_t, qg, kg, bd)


def _na_window_start(r0):
    return jnp.clip(r0 - NA_KH // 2, 0, GRID_ROWS - NA_WIN)


def _na_kernel(q_ref, k_ref, v_ref, bias_ref, o_ref):
    ws = _na_window_start(pl.program_id(1) * NA_ROWS_PER_STEP)
    krow = pl.multiple_of(ws * GRID_W, GRID_W)
    q = q_ref[0]
    kw = k_ref[0, pl.ds(krow, NA_WIN * GRID_W), :]
    vw = v_ref[0, pl.ds(krow, NA_WIN * GRID_W), :]
    outs = []
    for h in range(NA_HEADS):
        cs = slice(h * HEAD_DIM, (h + 1) * HEAD_DIM)
        s = _dot_nt(q[:, cs], kw[:, cs]) + bias_ref[0, h]
        m = jnp.max(s, -1, keepdims=True)
        e = jnp.exp(s - m)
        l = jnp.sum(e, -1, keepdims=True)
        outs.append(_dot(e.astype(bf16), vw[:, cs]) / l)
    o_ref[0] = jnp.concatenate(outs, -1).astype(bf16)


def _na_pattern(i):
    last = GRID_ROWS // NA_ROWS_PER_STEP - 1
    return jnp.where(i == 0, 0, jnp.where(i == last, 2, 1))


def _na(hna, bias_tab):
    batch = hna.shape[0]
    tq = NA_ROWS_PER_STEP * GRID_W
    width = NA_HEADS * HEAD_DIM
    return pl.pallas_call(
        _na_kernel, name="na", grid=(batch, SEQ // tq),
        in_specs=[pl.BlockSpec((1, tq, width), lambda b, i: (b, i, 0)),
                  pl.BlockSpec((1, SEQ, width), lambda b, i: (b, 0, 1)),
                  pl.BlockSpec((1, SEQ, width), lambda b, i: (b, 0, 2)),
                  pl.BlockSpec((1,) + bias_tab.shape[1:], lambda b, i: (_na_pattern(i), 0, 0, 0))],
        out_specs=pl.BlockSpec((1, tq, width), lambda b, i: (b, i, 0)),
        out_shape=jax.ShapeDtypeStruct((batch, SEQ, width), bf16),
        compiler_params=_cp(("parallel", "arbitrary")),
    )(hna, hna, hna, bias_tab)


def _na_bias_table(rpb):
    col = np.arange(GRID_W)
    col_start = np.clip(col - NA_KW // 2, 0, GRID_W - NA_KW)
    kc = np.arange(GRID_W)
    col_ok = (kc[None, :] >= col_start[:, None]) & (kc[None, :] < col_start[:, None] + NA_KW)
    dcol = np.clip(kc[None, :] - col[:, None] + NA_KW - 1, 0, 2 * NA_KW - 2)
    pick = jnp.asarray(dcol[None] == np.arange(2 * NA_KW - 1)[:, None, None], f32)
    tabs = []
    for r0 in (0, NA_ROWS_PER_STEP, GRID_ROWS - NA_ROWS_PER_STEP):
        ws = int(np.clip(r0 - NA_KH // 2, 0, GRID_ROWS - NA_WIN))
        r = r0 + np.arange(NA_ROWS_PER_STEP)
        rs = np.clip(r - NA_KH // 2, 0, GRID_ROWS - NA_KH)
        krow = ws + np.arange(NA_WIN)
        row_ok = (krow[None, :] >= rs[:, None]) & (krow[None, :] < rs[:, None] + NA_KH)
        drow = np.clip(krow[None, :] - r[:, None] + NA_KH - 1, 0, 2 * NA_KH - 2)
        g = jnp.einsum("hawk,kcd->hacwd", rpb[:, drow], pick, precision=lax.Precision.HIGHEST)
        ok = row_ok[:, None, :, None] & col_ok[None, :, None, :]
        tabs.append(jnp.where(ok[None], g, NEG).reshape(
            NA_HEADS, NA_ROWS_PER_STEP * GRID_W, NA_WIN * GRID_W))
    return jnp.stack(tabs).astype(f32)


def _gqa_kernel(q_ref, k_ref, v_ref, o_ref):
    g_per_kv = GQA_HEADS // GQA_KV
    q = q_ref[0]
    q4 = jnp.concatenate([q[:, g * HEAD_DIM:(g + 1) * HEAD_DIM] for g in range(g_per_kv)], axis=0)
    rows = g_per_kv * GQA_TQ

    m = jnp.full((rows, 1), NEG, f32)
    acc = jnp.zeros((rows, 2 * HEAD_DIM), f32)
    for j in range(SEQ // GQA_TK):
        kc = k_ref[0, 0, j * GQA_TK:(j + 1) * GQA_TK, :]
        vc = v_ref[0, 0, j * GQA_TK:(j + 1) * GQA_TK, :]
        s = _dot_nt(q4, kc)
        mn = jnp.maximum(m, jnp.max(s, -1, keepdims=True))
        acc = jnp.exp2(m - mn) * acc + _dot(jnp.exp2(s - mn).astype(bf16), vc)
        m = mn
    o = acc[:, :HEAD_DIM] / acc[:, HEAD_DIM:HEAD_DIM + 1]
    for g in range(g_per_kv):
        o_ref[0, :, g * HEAD_DIM:(g + 1) * HEAD_DIM] = o[g * GQA_TQ:(g + 1) * GQA_TQ].astype(bf16)


def _gqa(qn, kn, vb):
    batch = qn.shape[0]
    width = (GQA_HEADS // GQA_KV) * HEAD_DIM
    qspec = pl.BlockSpec((1, GQA_TQ, width), lambda b, kv, i: (b, i, kv))
    kvspec = lambda width: pl.BlockSpec((1, 1, SEQ, width), lambda b, kv, i: (b, kv, 0, 0))
    return pl.pallas_call(
        _gqa_kernel, name="gqa", grid=(batch, GQA_KV, SEQ // GQA_TQ),
        in_specs=[qspec, kvspec(HEAD_DIM), kvspec(2 * HEAD_DIM)], out_specs=qspec,
        out_shape=jax.ShapeDtypeStruct((batch, SEQ, GQA_HEADS * HEAD_DIM), bf16),
        compiler_params=_cp(("parallel", "parallel", "arbitrary")),
    )(qn, kn, vb)


def _sgu_kernel(h_ref, ws_ref, bs_ref, g_ref, b_ref, bd_ref, o_ref):
    tm = h_ref.shape[0]
    width = SG_GROUPS * HEAD_DIM
    bd = bd_ref[...]
    u = jax.nn.gelu(h_ref[:, :width])
    v = jax.nn.gelu(h_ref[:, width:])
    parts = []
    for j in range(width // LANES):
        x = v[:, j * LANES:(j + 1) * LANES]
        xc = x - _seg_mean(x, bd)
        parts.append(xc * lax.rsqrt(_seg_mean(xc * xc, bd) + LN_EPS))
    vn = (jnp.concatenate(parts, 1) * g_ref[...] + b_ref[...]).astype(bf16)
    bs = bs_ref[...]
    for c in range(tm // SG_CHUNK):
        rows = slice(c * SG_CHUNK, (c + 1) * SG_CHUNK)
        mixed = jnp.concatenate(
            [_dot(ws_ref[g], vn[rows, g * HEAD_DIM:(g + 1) * HEAD_DIM]) for g in range(SG_GROUPS)], axis=1)
        o_ref[rows, :] = (u[rows] * (mixed + bs)).astype(bf16)


def _sgu(hsg, ws, bs_tab, g, b, bd):
    n = hsg.shape[0]
    tm = ROW_TILE
    width = SG_GROUPS * HEAD_DIM
    const = lambda shape: pl.BlockSpec(shape, lambda i: tuple(0 for _ in shape))
    return pl.pallas_call(
        _sgu_kernel, name="sgu", grid=(n // tm,),
        in_specs=[pl.BlockSpec((tm, 2 * width), lambda i: (i, 0)), const(ws.shape), const(bs_tab.shape),
                  const((1, width)), const((1, width)), const((LANES, LANES))],
        out_specs=pl.BlockSpec((tm, width), lambda i: (i, 0)),
        out_shape=jax.ShapeDtypeStruct((n, width), bf16),
        compiler_params=_cp(("parallel",)),
    )(hsg, ws, bs_tab, g, b, bd)


def _outproj_kernel(ya_ref, yb_ref, yc_ref, x_ref, w_ref, g_ref, b_ref, wrh_ref, wrl_ref, x1_ref, aff_ref):
    mix = (_dot(ya_ref[...], w_ref[0:256]) + _dot(yb_ref[...], w_ref[256:768])
           + _dot(yc_ref[...], w_ref[768:1024]))
    x1 = _layer_norm(DN_ALPHA * x_ref[...] + mix, g_ref[...], b_ref[...])
    _store_slabs(x1_ref, x1)
    hi, lo = _split2(x1)
    both = _dot(hi, wrl_ref[...])
    logits = both[:, :N_EXPERTS] + both[:, N_EXPERTS:] + _dot(lo, wrh_ref[...])
    m = jnp.max(logits, -1, keepdims=True)
    e = jnp.exp(logits - m)
    aff_ref[...] = e / jnp.sum(e, -1, keepdims=True)


def _outproj(ya, yb, yc, x, w, g, b, wrh, wrl):
    n = x.shape[0]
    tm = ROW_TILE
    row = lambda width: pl.BlockSpec((tm, width), lambda i: (i, 0))
    const = lambda shape: pl.BlockSpec(shape, lambda i: tuple(0 for _ in shape))
    return pl.pallas_call(
        _outproj_kernel, name="outproj", grid=(n // tm,),
        in_specs=[row(256), row(512), row(256), row(D_MODEL), const(w.shape), const((1, D_MODEL)),
                  const((1, D_MODEL)), const(wrh.shape), const(wrl.shape)],
        out_specs=[pl.BlockSpec((tm * SLAB, LANES), lambda i: (i, 0)), row(N_EXPERTS)],
        out_shape=[jax.ShapeDtypeStruct((n * SLAB, LANES), f32), jax.ShapeDtypeStruct((n, N_EXPERTS), f32)],
        compiler_params=_cp(("parallel",)),
    )(ya, yb, yc, x, w, g, b, wrh, wrl)


def _tri_incl():
    r = lax.broadcasted_iota(i32, (LANES, LANES), 0)
    c = lax.broadcasted_iota(i32, (LANES, LANES), 1)
    return (r <= c).astype(bf16)


def _tri_strict_lower(nc):
    r = lax.broadcasted_iota(i32, (nc, nc), 0)
    c = lax.broadcasted_iota(i32, (nc, nc), 1)
    return (c < r).astype(bf16)


def _byte_planes(x, planes):
    xi = x.astype(i32)
    return [((xi >> (8 * k)) & 255).astype(f32).astype(bf16) for k in range(planes)]


def _token_prefix(m, planes):
    nc = m.shape[0]
    lincl = _dot(m.astype(bf16), _tri_incl())
    tot = jnp.broadcast_to(lincl[:, LANES - 1:LANES], m.shape)
    sl = _tri_strict_lower(nc)
    pref = sum(_dot(sl, p) * float(256 ** k) for k, p in enumerate(_byte_planes(tot, planes)))
    return lincl, tot, pref


def _route1_kernel(aff_ref, sel_ref, *, cap):
    bits = lax.bitcast_convert_type(aff_ref[0], i32)

    def body(i, t):
        cand = t | lax.shift_left(jnp.int32(1), 30 - i)
        cnt = jnp.sum((bits >= cand).astype(f32))
        return jnp.where(cnt >= cap, cand, t)

    thr = lax.fori_loop(0, 31, body, jnp.int32(0))
    gt = bits > thr
    eq = bits == thr
    need = cap - jnp.sum(gt.astype(f32))
    eqf = eq.astype(f32)
    lincl, _, pref = _token_prefix(eqf, 1)
    before = pref + lincl - eqf
    sel_ref[0] = (gt | (eq & (before < need))).astype(f32)


def _route1(aff_t, cap):
    ne, nc, _ = aff_t.shape
    spec = pl.BlockSpec((1, nc, LANES), lambda e: (e, 0, 0))
    return pl.pallas_call(
        functools.partial(_route1_kernel, cap=cap), name="route1", grid=(ne,), in_specs=[spec],
        out_specs=spec,
        out_shape=jax.ShapeDtypeStruct(aff_t.shape, f32), compiler_params=_cp(("parallel",)),
    )(aff_t)


def _route2_kernel(aff_ref, sel_ref, idx_ref, gate_ref, rho_ref, off_ref, cnt_ref, *, cap):
    e = pl.program_id(0)
    nc = aff_ref.shape[1]
    pw = ROUTE_PW

    count = jnp.zeros((nc, LANES), f32)
    rank = jnp.zeros((nc, LANES), f32)
    for k in range(N_EXPERTS):
        sk = sel_ref[k]
        count = count + sk
        rank = rank + jnp.where(k < e, sk, 0.0)
    lc, _, pc = _token_prefix(count, 2)
    off = pc + lc - count
    off_ref[...] = off
    cnt_ref[...] = count

    sel = sel_ref[e]
    lincl, tot, pref = _token_prefix(sel, 1)
    cse = pref[:, 0:1]
    csi = cse + tot[:, 0:1]
    chunk_id = lax.broadcasted_iota(i32, (nc, 1), 0).astype(f32)

    lt = lincl.T.astype(bf16)
    a = aff_ref[0].T
    a1 = a.astype(bf16)
    r1 = a - a1.astype(f32)
    a2 = r1.astype(bf16)
    a3 = (r1 - a2.astype(f32)).astype(bf16)
    dest = _byte_planes((off + rank).T, 3)
    lane_id = lax.broadcasted_iota(i32, (LANES, pw), 0).astype(f32)

    def tile(k, carry):
        p = (k * pw + lax.broadcasted_iota(i32, (1, pw), 1)).astype(f32)
        oh = (cse <= p) & (p < csi)
        ohb = oh.astype(f32).astype(bf16)
        chunk = jnp.sum(jnp.where(oh, chunk_id, 0.0), axis=0, keepdims=True)
        local = p - jnp.sum(jnp.where(oh, cse, 0.0), axis=0, keepdims=True)
        jstar = jnp.sum((_dot(lt, ohb) <= local).astype(f32), axis=0, keepdims=True)
        hit = lane_id == jstar
        g = _dot(a1, ohb) + _dot(a2, ohb) + _dot(a3, ohb)
        r = _dot(dest[0], ohb) + 256.0 * _dot(dest[1], ohb) + 65536.0 * _dot(dest[2], ohb)
        idx_ref[0, pl.ds(k, 1), :] = (chunk * LANES + jstar).astype(i32)
        gate_ref[0, pl.ds(k, 1), :] = jnp.sum(jnp.where(hit, g, 0.0), axis=0, keepdims=True)
        rho_ref[0, pl.ds(k, 1), :] = jnp.sum(jnp.where(hit, r, 0.0), axis=0, keepdims=True).astype(i32)
        return carry

    lax.fori_loop(0, cap // pw, tile, 0)


def _route2(aff_t, sel, cap):
    ne, nc, _ = aff_t.shape
    pw = ROUTE_PW
    slot = pl.BlockSpec((1, cap // pw, pw), lambda e: (e, 0, 0))
    tok = pl.BlockSpec((nc, LANES), lambda e: (0, 0))
    slot_shape = lambda dt: jax.ShapeDtypeStruct((ne, cap // pw, pw), dt)
    return pl.pallas_call(
        functools.partial(_route2_kernel, cap=cap), name="route2", grid=(ne,),
        in_specs=[pl.BlockSpec((1, nc, LANES), lambda e: (e, 0, 0)),
                  pl.BlockSpec((ne, nc, LANES), lambda e: (0, 0, 0))],
        out_specs=[slot, slot, slot, tok, tok],
        out_shape=[slot_shape(i32), slot_shape(f32), slot_shape(i32),
                   jax.ShapeDtypeStruct((nc, LANES), f32), jax.ShapeDtypeStruct((nc, LANES), f32)],
        compiler_params=_cp(("arbitrary",)),
    )(aff_t, sel)


def _row_to_col(row):
    n = row.shape[1]
    r = lax.broadcasted_iota(i32, (n, n), 0)
    c = lax.broadcasted_iota(i32, (n, n), 1)
    return jnp.sum(jnp.where(r == c, row, 0.0), axis=1, keepdims=True)


def _ffn_kernel(idx_sm, rho_sm, x_hbm, gate_ref, wg_ref, wu_ref, wd_ref, *rest, aliased):
    if aliased:
        _, z_hbm, xbuf, ybuf, gsem, ssem = rest
    else:
        z_hbm, xbuf, ybuf, gsem, ssem = rest
    tm = FFN_TM
    mt = pl.num_programs(1)
    total = pl.num_programs(0) * mt
    s = pl.program_id(0) * mt + pl.program_id(1)
    slot = s % 2

    def gather_copy(tok, i, sl):
        return pltpu.make_async_copy(x_hbm.at[pl.ds(pl.multiple_of(tok * SLAB, SLAB), SLAB), :],
                                     xbuf.at[sl, pl.ds(pl.multiple_of(i * SLAB, SLAB), SLAB), :], gsem.at[sl])

    def scatter_copy(row, i, sl):
        return pltpu.make_async_copy(ybuf.at[sl, pl.ds(pl.multiple_of(i * SLAB, SLAB), SLAB), :],
                                     z_hbm.at[pl.ds(pl.multiple_of(row * SLAB, SLAB), SLAB), :], ssem.at[sl])

    def start_gather(step, sl):
        def body(i, c):
            gather_copy(idx_sm[step * tm + i], i, sl).start()
            return c
        lax.fori_loop(0, tm, body, 0, unroll=DMA_UNROLL)

    def wait_gather(sl):
        pltpu.make_async_copy(x_hbm.at[pl.ds(0, tm * SLAB), :], xbuf.at[sl], gsem.at[sl]).wait()

    def wait_scatter(sl):
        pltpu.make_async_copy(ybuf.at[sl], z_hbm.at[pl.ds(0, tm * SLAB), :], ssem.at[sl]).wait()

    @pl.when(s == 0)
    def _():
        start_gather(0, 0)

    @pl.when(s + 1 < total)
    def _():
        start_gather(s + 1, 1 - slot)

    wait_gather(slot)

    @pl.when(s >= 2)
    def _():
        wait_scatter(slot)

    x = _load_slabs(xbuf.at[slot], tm).astype(bf16)
    y = jnp.zeros((tm, D_MODEL), f32)
    for c in range(D_EXPERT // FFN_FCHUNK):
        cols = slice(c * FFN_FCHUNK, (c + 1) * FFN_FCHUNK)
        hg = _dot(x, wg_ref[0, :, cols])
        hu = _dot(x, wu_ref[0, :, cols])
        y = y + _dot((jax.nn.silu(hg) * hu).astype(bf16), wd_ref[0, cols, :])
    _store_slabs(ybuf.at[slot], y * _row_to_col(gate_ref[0]))

    def start_scatter(i, c):
        scatter_copy(rho_sm[s * tm + i], i, slot).start()
        return c
    lax.fori_loop(0, tm, start_scatter, 0, unroll=DMA_UNROLL)

    @pl.when(s == total - 1)
    def _():
        wait_scatter(slot)

        @pl.when(total >= 2)
        def _():
            wait_scatter(1 - slot)


def _ffn_call(idx, rho, x1, gate, wg, wu, wd, z, e0, eg, cap):
    tm = FFN_TM
    mt = cap // tm
    n_rows = N_EXPERTS * cap
    aliased = z is not None
    any_spec = pl.BlockSpec(memory_space=pl.ANY)
    in_specs = [any_spec,
                pl.BlockSpec((1, 1, tm), lambda e, m, *_: ((e0 + e) * mt + m, 0, 0)),
                pl.BlockSpec((1, D_MODEL, D_EXPERT), lambda e, m, *_: (e0 + e, 0, 0)),
                pl.BlockSpec((1, D_MODEL, D_EXPERT), lambda e, m, *_: (e0 + e, 0, 0)),
                pl.BlockSpec((1, D_EXPERT, D_MODEL), lambda e, m, *_: (e0 + e, 0, 0))]
    args = [x1, gate, wg, wu, wd]
    if aliased:
        in_specs.append(any_spec)
        args.append(z)
    grid_spec = pltpu.PrefetchScalarGridSpec(
        num_scalar_prefetch=2, grid=(eg, mt), in_specs=in_specs, out_specs=any_spec,
        scratch_shapes=[pltpu.VMEM((2, tm * SLAB, LANES), f32), pltpu.VMEM((2, tm * SLAB, LANES), f32),
                        pltpu.SemaphoreType.DMA((2,)), pltpu.SemaphoreType.DMA((2,))])
    return pl.pallas_call(
        functools.partial(_ffn_kernel, aliased=aliased), name="ffn", grid_spec=grid_spec,
        out_shape=jax.ShapeDtypeStruct((n_rows * SLAB, LANES), f32),
        input_output_aliases={7: 0} if aliased else {},
        compiler_params=pltpu.CompilerParams(dimension_semantics=("arbitrary", "arbitrary"),
                                             vmem_limit_bytes=56 * MIB, has_side_effects=True),
    )(idx[e0:e0 + eg].reshape(-1), rho[e0:e0 + eg].reshape(-1), *args)


def _ffn(idx, rho, x1, gate, wg, wu, wd, cap):
    eg = max(1, min(N_EXPERTS, SMEM_INDEX_BYTES // (2 * 4 * cap)))
    gate3 = gate.reshape(N_EXPERTS * cap // FFN_TM, 1, FFN_TM)
    z = None
    for e0 in range(0, N_EXPERTS, eg):
        z = _ffn_call(idx, rho, x1, gate3, wg, wu, wd, z, e0, eg, cap)
    return z


def _combine_kernel(rs_sm, x_ref, off_ref, cnt_ref, g_ref, b_ref, z_hbm, o_ref, zbuf, sem, *, total_rows):
    tt = COMBINE_TT
    rc = COMBINE_RC
    i = pl.program_id(0)

    def tile_rows(t):
        first = (rs_sm[t] >> 3) << 3
        return first, (rs_sm[t + 1] - first + rc - 1) >> (rc.bit_length() - 1)

    r0, nch = tile_rows(i)

    offs = []
    ends = []
    for c in range(tt // LANES):
        o_row = off_ref[0, c:c + 1, :]
        offs.append(_row_to_col(o_row))
        ends.append(_row_to_col(o_row + cnt_ref[0, c:c + 1, :]))
    off_col = jnp.concatenate(offs, 0)
    end_col = jnp.concatenate(ends, 0)

    def chunk_start(first, j):
        return pl.multiple_of(jnp.minimum(first + j * rc, total_rows - rc), 8)

    def chunk_copy(first, j, sl):
        return pltpu.make_async_copy(z_hbm.at[pl.ds(chunk_start(first, j) * SLAB, rc * SLAB), :], zbuf.at[sl],
                                     sem.at[sl])

    @pl.when((i == 0) & (nch > 0))
    def _():
        chunk_copy(r0, 0, 0).start()

    def body(j, acc):
        sl = j % 2

        @pl.when(j + 1 < nch)
        def _():
            chunk_copy(r0, j + 1, 1 - sl).start()

        chunk_copy(r0, j, sl).wait()
        rowid = chunk_start(r0, j) + lax.broadcasted_iota(i32, (1, rc), 1)
        fresh = rowid >= r0 + j * rc
        rowf = rowid.astype(f32)
        seg = ((off_col <= rowf) & (rowf < end_col) & fresh).astype(f32).astype(bf16)
        hi, lo = _split2(_load_slabs(zbuf.at[sl], rc))
        return acc + _dot(seg, hi) + _dot(seg, lo)

    acc = lax.fori_loop(0, nch, body, jnp.zeros((tt, D_MODEL), f32))

    @pl.when(i + 1 < pl.num_programs(0))
    def _():
        nxt, nch_nxt = tile_rows(i + 1)

        @pl.when(nch_nxt > 0)
        def _():
            chunk_copy(nxt, 0, 0).start()

    o_ref[...] = _layer_norm(DN_ALPHA * _load_slabs(x_ref, tt) + acc, g_ref[...], b_ref[...])


def _combine(row_starts, x1, off, cnt, g, b, z):
    n = x1.shape[0] // SLAB
    tt = COMBINE_TT
    per = tt // LANES
    tok = pl.BlockSpec((1, per, LANES), lambda i, *_: (i, 0, 0))
    vec = pl.BlockSpec((1, D_MODEL), lambda i, *_: (0, 0))
    row = pl.BlockSpec((tt, D_MODEL), lambda i, *_: (i, 0))
    grid_spec = pltpu.PrefetchScalarGridSpec(
        num_scalar_prefetch=1, grid=(n // tt,),
        in_specs=[pl.BlockSpec((tt * SLAB, LANES), lambda i, *_: (i, 0)), tok, tok, vec, vec,
                  pl.BlockSpec(memory_space=pl.ANY)], out_specs=row,
        scratch_shapes=[pltpu.VMEM((2, COMBINE_RC * SLAB, LANES), f32), pltpu.SemaphoreType.DMA((2,))])
    return pl.pallas_call(
        functools.partial(_combine_kernel, total_rows=z.shape[0] // SLAB), name="combine",
        grid_spec=grid_spec,
        out_shape=jax.ShapeDtypeStruct((n, D_MODEL), f32), compiler_params=_cp(("arbitrary",)),
    )(row_starts, x1, off.reshape(n // tt, per, LANES), cnt.reshape(n // tt, per, LANES), g, b, z)


def _rope_tables():
    t = np.arange(SEQ)
    row = (t // GRID_W).astype(np.float32)
    col = (t % GRID_W).astype(np.float32)
    n_freq = HEAD_DIM // 4
    inv = jnp.asarray(ROPE_THETA, f32) ** (-jnp.arange(n_freq, dtype=f32) / n_freq)
    ang = jnp.concatenate([row[:, None] * inv, col[:, None] * inv], -1)
    cos = jnp.repeat(jnp.cos(ang), 2, axis=1)
    sin = jnp.repeat(jnp.sin(ang), 2, axis=1)
    sign = jnp.tile(jnp.asarray([-1.0, 1.0], f32), HEAD_DIM // 2)
    return jnp.tile(cos, (1, 2)), jnp.tile(sin * sign, (1, 2))


def _seg_matrix():
    r = np.arange(LANES)
    return jnp.asarray((r[:, None] // HEAD_DIM == r[None, :] // HEAD_DIM) / HEAD_DIM, bf16)


def _layer(x, batch, p, cos_t, sin_t, bd):
    n = x.shape[0]
    cap = 2 * n // N_EXPERTS
    hna, qn, kn, vb, hsg = _proj(x, p["w_in"], cos_t, sin_t, p["q_norm"], p["k_norm"], bd, batch)
    ya = _na(hna.reshape(batch, SEQ, -1), p["na_bias"]).reshape(n, -1)
    yb = _gqa(qn.reshape(batch, SEQ, -1), kn, vb).reshape(n, -1)
    yc = _sgu(hsg, p["sg_w"], p["sg_b"], p["sg_ln_g"], p["sg_ln_b"], bd)
    x1, aff = _outproj(ya, yb, yc, x, p["w_out"], p["ln1_g"], p["ln1_b"], p["wr_hi"], p["wr_lo"])
    aff_t = aff.T.reshape(N_EXPERTS, n // LANES, LANES)
    sel = _route1(aff_t, cap)
    idx, gate, rho, off, cnt = _route2(aff_t, sel, cap)
    z = _ffn(idx.reshape(N_EXPERTS, cap), rho.reshape(N_EXPERTS, cap), x1, gate,
             p["w_gate"], p["w_up"], p["w_down"], cap)
    row_starts = jnp.concatenate(
        [off.reshape(-1)[::COMBINE_TT], jnp.full((1,), z.shape[0] // SLAB, f32)]).astype(i32)
    return _combine(row_starts, x1, off, cnt, p["ln2_g"], p["ln2_b"], z)


def _trunk(x, ln_in_g, ln_in_b, layers, cos_t, sin_t, bd):
    batch = x.shape[0]
    h = _ln_in(x.reshape(batch * SEQ, D_MODEL), ln_in_g, ln_in_b)
    for p in layers:
        h = _layer(h, batch, p, cos_t, sin_t, bd)
    return h.reshape(batch, SEQ, D_MODEL)


def _layer_params(l, w_in, na_rpb, q_norm, k_norm, sg_w, sg_b, sg_ln_g, sg_ln_b, w_out, ln1_g, ln1_b,
                  w_router, w_gate, w_up, w_down, ln2_g, ln2_b):
    wr = w_router[l]
    wr_hi = wr.astype(bf16)
    width = SG_GROUPS * HEAD_DIM
    return {
        "w_in": w_in[l].astype(bf16),
        "na_bias": _na_bias_table(na_rpb[l]),
        "q_norm": jnp.tile(q_norm[l], 2).reshape(1, LANES),
        "k_norm": jnp.tile(k_norm[l], 2).reshape(1, LANES),
        "sg_w": sg_w[l].astype(bf16),
        "sg_b": jnp.repeat(sg_b[l].T, HEAD_DIM, axis=1),
        "sg_ln_g": sg_ln_g[l].reshape(1, width),
        "sg_ln_b": sg_ln_b[l].reshape(1, width),
        "w_out": w_out[l].astype(bf16),
        "ln1_g": ln1_g[l].reshape(1, -1), "ln1_b": ln1_b[l].reshape(1, -1),
        "wr_hi": wr_hi,
        "wr_lo": jnp.concatenate([wr_hi, (wr - wr_hi.astype(f32)).astype(bf16)], axis=1),
        "w_gate": w_gate[l].astype(bf16), "w_up": w_up[l].astype(bf16), "w_down": w_down[l].astype(bf16),
        "ln2_g": ln2_g[l].reshape(1, -1), "ln2_b": ln2_b[l].reshape(1, -1),
    }


def kernel(x_prompt, x_sample, ln_in_g, ln_in_b, w_in, na_rpb, q_norm, k_norm, sg_w, sg_b, sg_ln_g, sg_ln_b,
           w_out, ln1_g, ln1_b, w_router, w_gate, w_up, w_down, ln2_g, ln2_b):
    layers = [_layer_params(l, w_in, na_rpb, q_norm, k_norm, sg_w, sg_b, sg_ln_g, sg_ln_b, w_out, ln1_g,
                            ln1_b, w_router, w_gate, w_up, w_down, ln2_g, ln2_b)
              for l in range(w_in.shape[0])]
    cos_t, sin_t = _rope_tables()
    bd = _seg_matrix()
    return (_trunk(x_prompt, ln_in_g, ln_in_b, layers, cos_t, sin_t, bd),
            _trunk(x_sample, ln_in_g, ln_in_b, layers, cos_t, sin_t, bd))
```

```python
import functools

import numpy as np
import jax
import jax.numpy as jnp
from jax import lax
from jax.experimental import pallas as pl
from jax.experimental.pallas import tpu as pltpu

f32 = jnp.float32
bf16 = jnp.bfloat16
i32 = jnp.int32

D_MODEL = 1024
SEQ = 4096
DEPTH = 4
GRID_W = 64
GRID_ROWS = SEQ // GRID_W
HEAD_DIM = 64
NA_HEADS = 4
NA_KH = 8
NA_KW = 16
GQA_HEADS = 8
GQA_KV = 2
ROPE_THETA = 10000.0
SG_GROUPS = 4
SG_CHUNK = 128
N_EXPERTS = 16
D_EXPERT = 2048
LN_EPS = 1e-5
QK_EPS = 1e-6
DN_ALPHA = (2 * DEPTH) ** 0.25
QK_SCALE = HEAD_DIM ** -0.5
LOG2E = 1.4426950408889634

LANES = 128
NEG = -1e30
MIB = 1024 * 1024

NA_ROWS_PER_STEP = 4
NA_BLOCKS_PER_STEP = 2
NA_WIN = 12
GQA_TQ = 256
GQA_TK = 512
ROW_TILE = 512
FFN_TM = 512
FFN_FCHUNK = 512
DMA_UNROLL = 8
ROUTE_PW = 512
COMBINE_TT = 256
COMBINE_RC = 256
COMBINE_STATIC = 3
SMEM_INDEX_BYTES = 128 * 1024


def _cp(semantics, vmem_mib=48):
    return pltpu.CompilerParams(dimension_semantics=semantics, vmem_limit_bytes=vmem_mib * MIB)


def _dot(a, b):
    return jnp.dot(a, b, preferred_element_type=f32)


def _dot_nt(a, b):
    return lax.dot_general(a, b, (((1,), (1,)), ((), ())), preferred_element_type=f32)


def _layer_norm(x, g, b):
    mu = jnp.mean(x, -1, keepdims=True)
    xc = x - mu
    var = jnp.mean(xc * xc, -1, keepdims=True)
    return xc * lax.rsqrt(var + LN_EPS) * g + b


def _split2(x):
    hi = x.astype(bf16)
    lo = (x - hi.astype(f32)).astype(bf16)
    return hi, lo


SLAB = D_MODEL // LANES


def _load_slabs(ref, rows):
    return jnp.concatenate([ref[pl.ds(a, rows, stride=SLAB), :] for a in range(SLAB)], axis=1)


def _store_slabs(ref, x):
    rows = x.shape[0]
    for a in range(SLAB):
        ref[pl.ds(a, rows, stride=SLAB), :] = x[:, a * LANES:(a + 1) * LANES]


def _seg_mean(x, bd):
    hi, lo = _split2(x)
    return _dot(hi, bd) + _dot(lo, bd)


def _ln_in_kernel(x_ref, g_ref, b_ref, o_ref):
    o_ref[...] = _layer_norm(x_ref[...], g_ref[...], b_ref[...])


def _ln_in(x, g, b):
    n = x.shape[0]
    tm = 1024
    row = pl.BlockSpec((tm, D_MODEL), lambda i: (i, 0))
    vec = pl.BlockSpec((1, D_MODEL), lambda i: (0, 0))
    return pl.pallas_call(
        _ln_in_kernel, name="ln_in", grid=(n // tm,), in_specs=[row, vec, vec], out_specs=row,
        out_shape=jax.ShapeDtypeStruct((n, D_MODEL), f32), compiler_params=_cp(("parallel",)),
    )(x, g.reshape(1, -1), b.reshape(1, -1))


def _proj_kernel(x_ref, w_ref, cos_ref, sin_ref, qg_ref, kg_ref, bd_ref,
                 hna_ref, qn_ref, kn_ref, vb_ref, hsg_ref):
    tm = x_ref.shape[0]
    xb = x_ref[...].astype(bf16)

    wide = {}

    def mm(c):
        if c // 2 not in wide:
            wide[c // 2] = _dot(xb, w_ref[:, 512 * (c // 2):512 * (c // 2 + 1)])
        return wide[c // 2][:, 256 * (c % 2):256 * (c % 2 + 1)]

    hna_ref[:, 0:256] = (mm(0) * QK_SCALE).astype(bf16)
    hna_ref[:, 256:512] = mm(1).astype(bf16)
    hna_ref[:, 512:768] = mm(2).astype(bf16)

    cos = cos_ref[...]
    sin = sin_ref[...]
    bd = bd_ref[...]
    even = (lax.broadcasted_iota(i32, (tm, LANES), 1) & 1) == 0

    def norm_rope(x, g):
        y = x * lax.rsqrt(_seg_mean(x * x, bd) + QK_EPS) * g
        partner = jnp.where(even, pltpu.roll(y, LANES - 1, 1), pltpu.roll(y, 1, 1))
        return y * cos + partner * sin

    qg = qg_ref[...]
    for c in (3, 4):
        a = mm(c)
        for j in range(2):
            col = (c - 3) * 256 + j * LANES
            qn_ref[:, col:col + LANES] = (
                norm_rope(a[:, j * LANES:(j + 1) * LANES], qg) * (QK_SCALE * LOG2E)).astype(bf16)
    a = mm(5)
    kn = norm_rope(a[:, :LANES], kg_ref[...]).astype(bf16)
    kn_ref[0, 0] = kn[:, :HEAD_DIM]
    kn_ref[0, 1] = kn[:, HEAD_DIM:]
    vb = a[:, LANES:].astype(bf16)
    ones = jnp.ones((tm, HEAD_DIM), bf16)
    vb_ref[0, 0] = jnp.concatenate([vb[:, :HEAD_DIM], ones], axis=1)
    vb_ref[0, 1] = jnp.concatenate([vb[:, HEAD_DIM:], ones], axis=1)
    hsg_ref[:, :256] = mm(6)
    hsg_ref[:, 256:] = mm(7)


def _proj(x, w, cos_t, sin_t, qg, kg, bd, batch):
    n = x.shape[0]
    tm = ROW_TILE
    per_seq = SEQ // tm
    row = lambda width: pl.BlockSpec((tm, width), lambda i: (i, 0))
    const = lambda shape: pl.BlockSpec(shape, lambda i: tuple(0 for _ in shape))
    tab = pl.BlockSpec((tm, LANES), lambda i: (i % per_seq, 0))
    head_major = lambda width: pl.BlockSpec((1, GQA_KV, tm, width),
                                            lambda i: (i // per_seq, 0, i % per_seq, 0))
    return pl.pallas_call(
        _proj_kernel, name="proj", grid=(n // tm,),
        in_specs=[row(D_MODEL), const(w.shape), tab, tab, const((1, LANES)), const((1, LANES)),
                  const((LANES, LANES))],
        out_specs=[row(768), row(512), head_major(HEAD_DIM), head_major(2 * HEAD_DIM), row(512)],
        out_shape=[jax.ShapeDtypeStruct((n, 768), bf16),
                   jax.ShapeDtypeStruct((n, 512), bf16),
                   jax.ShapeDtypeStruct((batch, GQA_KV, SEQ, HEAD_DIM), bf16),
                   jax.ShapeDtypeStruct((batch, GQA_KV, SEQ, 2 * HEAD_DIM), bf16),
                   jax.ShapeDtypeStruct((n, 512), f32)],
        compiler_params=_cp(("parallel",)),
    )(x, w, cos_t, sin_t, qg, kg, bd)


def _na_window_start(r0):
    return jnp.clip(r0 - NA_KH // 2, 0, GRID_ROWS - NA_WIN)


def _na_kernel(q_ref, k_ref, v_ref, *rest):
    bias_refs, o_ref = rest[:-1], rest[-1]
    tq = NA_ROWS_PER_STEP * GRID_W
    for blk, bias_ref in enumerate(bias_refs):
        r0 = (pl.program_id(1) * NA_BLOCKS_PER_STEP + blk) * NA_ROWS_PER_STEP
        krow = pl.multiple_of(_na_window_start(r0) * GRID_W, GRID_W)
        q = q_ref[0, blk * tq:(blk + 1) * tq, :]
        kw = k_ref[0, pl.ds(krow, NA_WIN * GRID_W), :]
        vw = v_ref[0, pl.ds(krow, NA_WIN * GRID_W), :]
        outs = []
        for h in range(NA_HEADS):
            cs = slice(h * HEAD_DIM, (h + 1) * HEAD_DIM)
            s = _dot_nt(q[:, cs], kw[:, cs]) + bias_ref[0, h]
            m = jnp.max(s, -1, keepdims=True)
            e = jnp.exp(s - m)
            l = jnp.sum(e, -1, keepdims=True)
            outs.append(_dot(e.astype(bf16), vw[:, cs]) / l)
        o_ref[0, blk * tq:(blk + 1) * tq, :] = jnp.concatenate(outs, -1).astype(bf16)


def _na_pattern(block):
    last = GRID_ROWS // NA_ROWS_PER_STEP - 1
    return jnp.where(block == 0, 0, jnp.where(block == last, 2, 1))


def _na(hna, bias_tab):
    batch = hna.shape[0]
    nb = NA_BLOCKS_PER_STEP
    tq = nb * NA_ROWS_PER_STEP * GRID_W
    width = NA_HEADS * HEAD_DIM
    bias_spec = lambda blk: pl.BlockSpec((1,) + bias_tab.shape[1:],
                                         lambda b, i: (_na_pattern(i * nb + blk), 0, 0, 0))
    return pl.pallas_call(
        _na_kernel, name="na", grid=(batch, SEQ // tq),
        in_specs=[pl.BlockSpec((1, tq, width), lambda b, i: (b, i, 0)),
                  pl.BlockSpec((1, SEQ, width), lambda b, i: (b, 0, 1)),
                  pl.BlockSpec((1, SEQ, width), lambda b, i: (b, 0, 2))]
                 + [bias_spec(blk) for blk in range(nb)],
        out_specs=pl.BlockSpec((1, tq, width), lambda b, i: (b, i, 0)),
        out_shape=jax.ShapeDtypeStruct((batch, SEQ, width), bf16),
        compiler_params=_cp(("parallel", "arbitrary")),
    )(hna, hna, hna, *([bias_tab] * nb))


def _na_bias_table(rpb):
    col = np.arange(GRID_W)
    col_start = np.clip(col - NA_KW // 2, 0, GRID_W - NA_KW)
    kc = np.arange(GRID_W)
    col_ok = (kc[None, :] >= col_start[:, None]) & (kc[None, :] < col_start[:, None] + NA_KW)
    dcol = np.clip(kc[None, :] - col[:, None] + NA_KW - 1, 0, 2 * NA_KW - 2)
    pick = jnp.asarray(dcol[None] == np.arange(2 * NA_KW - 1)[:, None, None], f32)
    tabs = []
    for r0 in (0, NA_ROWS_PER_STEP, GRID_ROWS - NA_ROWS_PER_STEP):
        ws = int(np.clip(r0 - NA_KH // 2, 0, GRID_ROWS - NA_WIN))
        r = r0 + np.arange(NA_ROWS_PER_STEP)
        rs = np.clip(r - NA_KH // 2, 0, GRID_ROWS - NA_KH)
        krow = ws + np.arange(NA_WIN)
        row_ok = (krow[None, :] >= rs[:, None]) & (krow[None, :] < rs[:, None] + NA_KH)
        drow = np.clip(krow[None, :] - r[:, None] + NA_KH - 1, 0, 2 * NA_KH - 2)
        g = jnp.einsum("hawk,kcd->hacwd", rpb[:, drow], pick, precision=lax.Precision.HIGHEST)
        ok = row_ok[:, None, :, None] & col_ok[None, :, None, :]
        tabs.append(jnp.where(ok[None], g, NEG).reshape(
            NA_HEADS, NA_ROWS_PER_STEP * GRID_W, NA_WIN * GRID_W))
    return jnp.stack(tabs).astype(f32)


def _gqa_kernel(q_ref, k_ref, v_ref, o_ref):
    g_per_kv = GQA_HEADS // GQA_KV
    q = q_ref[0]
    q4 = jnp.concatenate([q[:, g * HEAD_DIM:(g + 1) * HEAD_DIM] for g in range(g_per_kv)], axis=0)
    rows = g_per_kv * GQA_TQ

    m = jnp.full((rows, 1), NEG, f32)
    acc = jnp.zeros((rows, 2 * HEAD_DIM), f32)
    for j in range(SEQ // GQA_TK):
        kc = k_ref[0, 0, j * GQA_TK:(j + 1) * GQA_TK, :]
        vc = v_ref[0, 0, j * GQA_TK:(j + 1) * GQA_TK, :]
        s = _dot_nt(q4, kc)
        mn = jnp.maximum(m, jnp.max(s, -1, keepdims=True))
        acc = jnp.exp2(m - mn) * acc + _dot(jnp.exp2(s - mn).astype(bf16), vc)
        m = mn
    o = acc[:, :HEAD_DIM] / acc[:, HEAD_DIM:HEAD_DIM + 1]
    for g in range(g_per_kv):
        o_ref[0, :, g * HEAD_DIM:(g + 1) * HEAD_DIM] = o[g * GQA_TQ:(g + 1) * GQA_TQ].astype(bf16)


def _gqa(qn, kn, vb):
    batch = qn.shape[0]
    width = (GQA_HEADS // GQA_KV) * HEAD_DIM
    qspec = pl.BlockSpec((1, GQA_TQ, width), lambda b, kv, i: (b, i, kv))
    kvspec = lambda width: pl.BlockSpec((1, 1, SEQ, width), lambda b, kv, i: (b, kv, 0, 0))
    return pl.pallas_call(
        _gqa_kernel, name="gqa", grid=(batch, GQA_KV, SEQ // GQA_TQ),
        in_specs=[qspec, kvspec(HEAD_DIM), kvspec(2 * HEAD_DIM)], out_specs=qspec,
        out_shape=jax.ShapeDtypeStruct((batch, SEQ, GQA_HEADS * HEAD_DIM), bf16),
        compiler_params=_cp(("parallel", "parallel", "arbitrary")),
    )(qn, kn, vb)


def _sgu_kernel(h_ref, ws_ref, bs_ref, g_ref, b_ref, bd_ref, o_ref):
    tm = h_ref.shape[0]
    width = SG_GROUPS * HEAD_DIM
    bd = bd_ref[...]
    u = jax.nn.gelu(h_ref[:, :width])
    v = jax.nn.gelu(h_ref[:, width:])
    parts = []
    for j in range(width // LANES):
        x = v[:, j * LANES:(j + 1) * LANES]
        xc = x - _seg_mean(x, bd)
        parts.append(xc * lax.rsqrt(_seg_mean(xc * xc, bd) + LN_EPS))
    vn = (jnp.concatenate(parts, 1) * g_ref[...] + b_ref[...]).astype(bf16)
    bs = bs_ref[...]
    for c in range(tm // SG_CHUNK):
        rows = slice(c * SG_CHUNK, (c + 1) * SG_CHUNK)
        mixed = jnp.concatenate(
            [_dot(ws_ref[g], vn[rows, g * HEAD_DIM:(g + 1) * HEAD_DIM]) for g in range(SG_GROUPS)], axis=1)
        o_ref[rows, :] = (u[rows] * (mixed + bs)).astype(bf16)


def _sgu(hsg, ws, bs_tab, g, b, bd):
    n = hsg.shape[0]
    tm = ROW_TILE
    width = SG_GROUPS * HEAD_DIM
    const = lambda shape: pl.BlockSpec(shape, lambda i: tuple(0 for _ in shape))
    return pl.pallas_call(
        _sgu_kernel, name="sgu", grid=(n // tm,),
        in_specs=[pl.BlockSpec((tm, 2 * width), lambda i: (i, 0)), const(ws.shape), const(bs_tab.shape),
                  const((1, width)), const((1, width)), const((LANES, LANES))],
        out_specs=pl.BlockSpec((tm, width), lambda i: (i, 0)),
        out_shape=jax.ShapeDtypeStruct((n, width), bf16),
        compiler_params=_cp(("parallel",)),
    )(hsg, ws, bs_tab, g, b, bd)


def _outproj_kernel(ya_ref, yb_ref, yc_ref, x_ref, w_ref, g_ref, b_ref, wrh_ref, wrl_ref, x1_ref, aff_ref):
    mix = (_dot(ya_ref[...], w_ref[0:256]) + _dot(yb_ref[...], w_ref[256:768])
           + _dot(yc_ref[...], w_ref[768:1024]))
    x1 = _layer_norm(DN_ALPHA * x_ref[...] + mix, g_ref[...], b_ref[...])
    _store_slabs(x1_ref, x1)
    hi, lo = _split2(x1)
    both = _dot_nt(wrl_ref[...], hi)
    logits = both[:N_EXPERTS] + both[N_EXPERTS:] + _dot_nt(wrh_ref[...], lo)
    m = jnp.max(logits, 0, keepdims=True)
    e = jnp.exp(logits - m)
    aff_ref[...] = e / jnp.sum(e, 0, keepdims=True)


def _outproj(ya, yb, yc, x, w, g, b, wrh, wrl):
    n = x.shape[0]
    tm = ROW_TILE
    row = lambda width: pl.BlockSpec((tm, width), lambda i: (i, 0))
    const = lambda shape: pl.BlockSpec(shape, lambda i: tuple(0 for _ in shape))
    return pl.pallas_call(
        _outproj_kernel, name="outproj", grid=(n // tm,),
        in_specs=[row(256), row(512), row(256), row(D_MODEL), const(w.shape), const((1, D_MODEL)),
                  const((1, D_MODEL)), const(wrh.shape), const(wrl.shape)],
        out_specs=[pl.BlockSpec((tm * SLAB, LANES), lambda i: (i, 0)),
                   pl.BlockSpec((N_EXPERTS, tm), lambda i: (0, i))],
        out_shape=[jax.ShapeDtypeStruct((n * SLAB, LANES), f32), jax.ShapeDtypeStruct((N_EXPERTS, n), f32)],
        compiler_params=_cp(("parallel",)),
    )(ya, yb, yc, x, w, g, b, wrh, wrl)


def _tri_incl():
    r = lax.broadcasted_iota(i32, (LANES, LANES), 0)
    c = lax.broadcasted_iota(i32, (LANES, LANES), 1)
    return (r <= c).astype(bf16)


def _tri_strict_lower(nc):
    r = lax.broadcasted_iota(i32, (nc, nc), 0)
    c = lax.broadcasted_iota(i32, (nc, nc), 1)
    return (c < r).astype(bf16)


def _byte_planes(x, planes):
    xi = x.astype(i32)
    return [((xi >> (8 * k)) & 255).astype(f32).astype(bf16) for k in range(planes)]


def _token_prefix(m, planes):
    nc = m.shape[0]
    lincl = _dot(m.astype(bf16), _tri_incl())
    tot = jnp.broadcast_to(lincl[:, LANES - 1:LANES], m.shape)
    sl = _tri_strict_lower(nc)
    pref = sum(_dot(sl, p) * float(256 ** k) for k, p in enumerate(_byte_planes(tot, planes)))
    return lincl, tot, pref


def _route1_kernel(aff_ref, sel_ref, *, cap):
    bits = lax.bitcast_convert_type(aff_ref[0], i32)

    def body(i, t):
        cand = t | lax.shift_left(jnp.int32(1), 30 - i)
        cnt = jnp.sum((bits >= cand).astype(f32))
        return jnp.where(cnt >= cap, cand, t)

    thr = lax.fori_loop(0, 31, body, jnp.int32(0))
    gt = bits > thr
    eq = bits == thr
    need = cap - jnp.sum(gt.astype(f32))
    eqf = eq.astype(f32)
    lincl, _, pref = _token_prefix(eqf, 1)
    before = pref + lincl - eqf
    sel_ref[0] = (gt | (eq & (before < need))).astype(f32)


def _route1(aff_t, cap):
    ne, nc, _ = aff_t.shape
    spec = pl.BlockSpec((1, nc, LANES), lambda e: (e, 0, 0))
    return pl.pallas_call(
        functools.partial(_route1_kernel, cap=cap), name="route1", grid=(ne,), in_specs=[spec],
        out_specs=spec,
        out_shape=jax.ShapeDtypeStruct(aff_t.shape, f32), compiler_params=_cp(("parallel",)),
    )(aff_t)


def _route2_kernel(aff_ref, sel_ref, idx_ref, gate_ref, rho_ref, off_ref, cnt_ref, *, cap):
    e = pl.program_id(0)
    nc = aff_ref.shape[1]
    pw = ROUTE_PW

    count = jnp.zeros((nc, LANES), f32)
    rank = jnp.zeros((nc, LANES), f32)
    for k in range(N_EXPERTS):
        sk = sel_ref[k]
        count = count + sk
        rank = rank + jnp.where(k < e, sk, 0.0)
    lc, _, pc = _token_prefix(count, 2)
    off = pc + lc - count
    off_ref[...] = off
    cnt_ref[...] = count

    sel = sel_ref[e]
    lincl, tot, pref = _token_prefix(sel, 1)
    cse = pref[:, 0:1]
    csi = cse + tot[:, 0:1]
    chunk_id = lax.broadcasted_iota(i32, (nc, 1), 0).astype(f32)

    lt = lincl.T.astype(bf16)
    a = aff_ref[0].T
    a1 = a.astype(bf16)
    r1 = a - a1.astype(f32)
    a2 = r1.astype(bf16)
    a3 = (r1 - a2.astype(f32)).astype(bf16)
    dest = _byte_planes((off + rank).T, 3)
    tables = jnp.concatenate([lt, a1, a2, a3] + dest, axis=0)
    lane_id = lax.broadcasted_iota(i32, (LANES, pw), 0).astype(f32)

    def tile(k, carry):
        p = (k * pw + lax.broadcasted_iota(i32, (1, pw), 1)).astype(f32)
        oh = (cse <= p) & (p < csi)
        ohb = oh.astype(f32).astype(bf16)
        chunk = jnp.sum(jnp.where(oh, chunk_id, 0.0), axis=0, keepdims=True)
        local = p - jnp.sum(jnp.where(oh, cse, 0.0), axis=0, keepdims=True)
        looked = _dot(tables, ohb)
        part = lambda t: looked[t * LANES:(t + 1) * LANES]
        jstar = jnp.sum((part(0) <= local).astype(f32), axis=0, keepdims=True)
        hit = lane_id == jstar
        g = part(1) + part(2) + part(3)
        r = part(4) + 256.0 * part(5) + 65536.0 * part(6)
        idx_ref[0, pl.ds(k, 1), :] = (chunk * LANES + jstar).astype(i32)
        gate_ref[0, pl.ds(k, 1), :] = jnp.sum(jnp.where(hit, g, 0.0), axis=0, keepdims=True)
        rho_ref[0, pl.ds(k, 1), :] = jnp.sum(jnp.where(hit, r, 0.0), axis=0, keepdims=True).astype(i32)
        return carry

    lax.fori_loop(0, cap // pw, tile, 0)


def _route2(aff_t, sel, cap):
    ne, nc, _ = aff_t.shape
    pw = ROUTE_PW
    slot = pl.BlockSpec((1, cap // pw, pw), lambda e: (e, 0, 0))
    tok = pl.BlockSpec((nc, LANES), lambda e: (0, 0))
    slot_shape = lambda dt: jax.ShapeDtypeStruct((ne, cap // pw, pw), dt)
    return pl.pallas_call(
        functools.partial(_route2_kernel, cap=cap), name="route2", grid=(ne,),
        in_specs=[pl.BlockSpec((1, nc, LANES), lambda e: (e, 0, 0)),
                  pl.BlockSpec((ne, nc, LANES), lambda e: (0, 0, 0))],
        out_specs=[slot, slot, slot, tok, tok],
        out_shape=[slot_shape(i32), slot_shape(f32), slot_shape(i32),
                   jax.ShapeDtypeStruct((nc, LANES), f32), jax.ShapeDtypeStruct((nc, LANES), f32)],
        compiler_params=_cp(("arbitrary",)),
    )(aff_t, sel)


def _row_to_col(row):
    n = row.shape[1]
    r = lax.broadcasted_iota(i32, (n, n), 0)
    c = lax.broadcasted_iota(i32, (n, n), 1)
    return jnp.sum(jnp.where(r == c, row, 0.0), axis=1, keepdims=True)


def _ffn_kernel(idx_sm, rho_sm, x_hbm, gate_ref, wg_ref, wu_ref, wd_ref, *rest, aliased):
    if aliased:
        _, z_hbm, xbuf, ybuf, gsem, ssem = rest
    else:
        z_hbm, xbuf, ybuf, gsem, ssem = rest
    tm = FFN_TM
    mt = pl.num_programs(1)
    total = pl.num_programs(0) * mt
    s = pl.program_id(0) * mt + pl.program_id(1)
    slot = s % 2

    def gather_copy(tok, i, sl):
        return pltpu.make_async_copy(x_hbm.at[pl.ds(pl.multiple_of(tok * SLAB, SLAB), SLAB), :],
                                     xbuf.at[sl, pl.ds(pl.multiple_of(i * SLAB, SLAB), SLAB), :], gsem.at[sl])

    def scatter_copy(row, i, sl):
        return pltpu.make_async_copy(ybuf.at[sl, pl.ds(pl.multiple_of(i * SLAB, SLAB), SLAB), :],
                                     z_hbm.at[pl.ds(pl.multiple_of(row * SLAB, SLAB), SLAB), :], ssem.at[sl])

    def start_gather(step, sl):
        def body(i, c):
            gather_copy(idx_sm[step * tm + i], i, sl).start()
            return c
        lax.fori_loop(0, tm, body, 0, unroll=DMA_UNROLL)

    def wait_gather(sl):
        pltpu.make_async_copy(x_hbm.at[pl.ds(0, tm * SLAB), :], xbuf.at[sl], gsem.at[sl]).wait()

    def wait_scatter(sl):
        pltpu.make_async_copy(ybuf.at[sl], z_hbm.at[pl.ds(0, tm * SLAB), :], ssem.at[sl]).wait()

    @pl.when(s == 0)
    def _():
        start_gather(0, 0)

    @pl.when(s + 1 < total)
    def _():
        start_gather(s + 1, 1 - slot)

    wait_gather(slot)

    @pl.when(s >= 2)
    def _():
        wait_scatter(slot)

    x = _load_slabs(xbuf.at[slot], tm).astype(bf16)
    y = jnp.zeros((tm, D_MODEL), f32)
    for c in range(D_EXPERT // FFN_FCHUNK):
        cols = slice(c * FFN_FCHUNK, (c + 1) * FFN_FCHUNK)
        hg = _dot(x, wg_ref[0, :, cols])
        hu = _dot(x, wu_ref[0, :, cols])
        y = y + _dot((jax.nn.silu(hg) * hu).astype(bf16), wd_ref[0, cols, :])
    _store_slabs(ybuf.at[slot], y * _row_to_col(gate_ref[0]))

    def start_scatter(i, c):
        scatter_copy(rho_sm[s * tm + i], i, slot).start()
        return c
    lax.fori_loop(0, tm, start_scatter, 0, unroll=DMA_UNROLL)

    @pl.when(s == total - 1)
    def _():
        wait_scatter(slot)

        @pl.when(total >= 2)
        def _():
            wait_scatter(1 - slot)


def _ffn_call(idx, rho, x1, gate, wg, wu, wd, z, e0, eg, cap):
    tm = FFN_TM
    mt = cap // tm
    n_rows = N_EXPERTS * cap
    aliased = z is not None
    any_spec = pl.BlockSpec(memory_space=pl.ANY)
    in_specs = [any_spec,
                pl.BlockSpec((1, 1, tm), lambda e, m, *_: ((e0 + e) * mt + m, 0, 0)),
                pl.BlockSpec((1, D_MODEL, D_EXPERT), lambda e, m, *_: (e0 + e, 0, 0)),
                pl.BlockSpec((1, D_MODEL, D_EXPERT), lambda e, m, *_: (e0 + e, 0, 0)),
                pl.BlockSpec((1, D_EXPERT, D_MODEL), lambda e, m, *_: (e0 + e, 0, 0))]
    args = [x1, gate, wg, wu, wd]
    if aliased:
        in_specs.append(any_spec)
        args.append(z)
    grid_spec = pltpu.PrefetchScalarGridSpec(
        num_scalar_prefetch=2, grid=(eg, mt), in_specs=in_specs, out_specs=any_spec,
        scratch_shapes=[pltpu.VMEM((2, tm * SLAB, LANES), f32), pltpu.VMEM((2, tm * SLAB, LANES), f32),
                        pltpu.SemaphoreType.DMA((2,)), pltpu.SemaphoreType.DMA((2,))])
    return pl.pallas_call(
        functools.partial(_ffn_kernel, aliased=aliased), name="ffn", grid_spec=grid_spec,
        out_shape=jax.ShapeDtypeStruct((n_rows * SLAB, LANES), f32),
        input_output_aliases={7: 0} if aliased else {},
        compiler_params=pltpu.CompilerParams(dimension_semantics=("arbitrary", "arbitrary"),
                                             vmem_limit_bytes=56 * MIB, has_side_effects=True),
    )(idx[e0:e0 + eg].reshape(-1), rho[e0:e0 + eg].reshape(-1), *args)


def _ffn(idx, rho, x1, gate, wg, wu, wd, cap):
    eg = max(1, min(N_EXPERTS, SMEM_INDEX_BYTES // (2 * 4 * cap)))
    gate3 = gate.reshape(N_EXPERTS * cap // FFN_TM, 1, FFN_TM)
    z = None
    for e0 in range(0, N_EXPERTS, eg):
        z = _ffn_call(idx, rho, x1, gate3, wg, wu, wd, z, e0, eg, cap)
    return z


def _combine_kernel(rs_sm, x_ref, off_ref, cnt_ref, g_ref, b_ref, z_hbm, o_ref, zbuf, sem, *, total_rows):
    tt = COMBINE_TT
    rc = COMBINE_RC
    i = pl.program_id(0)

    def tile_rows(t):
        first = (rs_sm[t] >> 3) << 3
        return first, (rs_sm[t + 1] - first + rc - 1) >> (rc.bit_length() - 1)

    r0, nch = tile_rows(i)

    offs = []
    ends = []
    for c in range(tt // LANES):
        o_row = off_ref[0, c:c + 1, :]
        offs.append(_row_to_col(o_row))
        ends.append(_row_to_col(o_row + cnt_ref[0, c:c + 1, :]))
    off_col = jnp.concatenate(offs, 0)
    end_col = jnp.concatenate(ends, 0)

    def chunk_start(first, j):
        return pl.multiple_of(jnp.minimum(first + j * rc, total_rows - rc), 8)

    def chunk_copy(first, j, sl):
        return pltpu.make_async_copy(z_hbm.at[pl.ds(chunk_start(first, j) * SLAB, rc * SLAB), :], zbuf.at[sl],
                                     sem.at[sl])

    def contribution(j, sl):
        rowid = chunk_start(r0, j) + lax.broadcasted_iota(i32, (1, rc), 1)
        fresh = rowid >= r0 + j * rc
        rowf = rowid.astype(f32)
        seg = ((off_col <= rowf) & (rowf < end_col) & fresh).astype(f32).astype(bf16)
        hi, lo = _split2(_load_slabs(zbuf.at[sl], rc))
        return _dot(seg, hi) + _dot(seg, lo)

    last = pl.num_programs(0) - 1
    nxt, _ = tile_rows(jnp.minimum(i + 1, last))

    cur = (i % 2) * COMBINE_STATIC
    oth = COMBINE_STATIC - cur

    @pl.when(i == 0)
    def _():
        for j in range(COMBINE_STATIC):
            chunk_copy(r0, j, cur + j).start()

    for j in range(COMBINE_STATIC):
        chunk_copy(r0, j, cur + j).wait()
    for j in range(COMBINE_STATIC):
        chunk_copy(nxt, j, oth + j).start()

    acc = jnp.zeros((tt, D_MODEL), f32)
    for j in range(COMBINE_STATIC):
        acc = acc + contribution(j, cur + j)

    def tail(j, acc):
        cp = chunk_copy(r0, j, 2 * COMBINE_STATIC)
        cp.start()
        cp.wait()
        return acc + contribution(j, 2 * COMBINE_STATIC)

    acc = lax.fori_loop(COMBINE_STATIC, jnp.maximum(nch, COMBINE_STATIC), tail, acc)

    @pl.when(i == last)
    def _():
        for j in range(COMBINE_STATIC):
            chunk_copy(nxt, j, oth + j).wait()

    o_ref[...] = _layer_norm(DN_ALPHA * _load_slabs(x_ref, tt) + acc, g_ref[...], b_ref[...])


def _combine(row_starts, x1, off, cnt, g, b, z):
    n = x1.shape[0] // SLAB
    tt = COMBINE_TT
    per = tt // LANES
    tok = pl.BlockSpec((1, per, LANES), lambda i, *_: (i, 0, 0))
    vec = pl.BlockSpec((1, D_MODEL), lambda i, *_: (0, 0))
    row = pl.BlockSpec((tt, D_MODEL), lambda i, *_: (i, 0))
    grid_spec = pltpu.PrefetchScalarGridSpec(
        num_scalar_prefetch=1, grid=(n // tt,),
        in_specs=[pl.BlockSpec((tt * SLAB, LANES), lambda i, *_: (i, 0)), tok, tok, vec, vec,
                  pl.BlockSpec(memory_space=pl.ANY)], out_specs=row,
        scratch_shapes=[pltpu.VMEM((2 * COMBINE_STATIC + 1, COMBINE_RC * SLAB, LANES), f32),
                        pltpu.SemaphoreType.DMA((2 * COMBINE_STATIC + 1,))])
    return pl.pallas_call(
        functools.partial(_combine_kernel, total_rows=z.shape[0] // SLAB), name="combine",
        grid_spec=grid_spec,
        out_shape=jax.ShapeDtypeStruct((n, D_MODEL), f32), compiler_params=_cp(("arbitrary",)),
    )(row_starts, x1, off.reshape(n // tt, per, LANES), cnt.reshape(n // tt, per, LANES), g, b, z)


def _rope_tables():
    t = np.arange(SEQ)
    row = (t // GRID_W).astype(np.float32)
    col = (t % GRID_W).astype(np.float32)
    n_freq = HEAD_DIM // 4
    inv = jnp.asarray(ROPE_THETA, f32) ** (-jnp.arange(n_freq, dtype=f32) / n_freq)
    ang = jnp.concatenate([row[:, None] * inv, col[:, None] * inv], -1)
    cos = jnp.repeat(jnp.cos(ang), 2, axis=1)
    sin = jnp.repeat(jnp.sin(ang), 2, axis=1)
    sign = jnp.tile(jnp.asarray([-1.0, 1.0], f32), HEAD_DIM // 2)
    return jnp.tile(cos, (1, 2)), jnp.tile(sin * sign, (1, 2))


def _seg_matrix():
    r = np.arange(LANES)
    return jnp.asarray((r[:, None] // HEAD_DIM == r[None, :] // HEAD_DIM) / HEAD_DIM, bf16)


def _layer(x, batch, p, cos_t, sin_t, bd):
    n = x.shape[0]
    cap = 2 * n // N_EXPERTS
    hna, qn, kn, vb, hsg = _proj(x, p["w_in"], cos_t, sin_t, p["q_norm"], p["k_norm"], bd, batch)
    ya = _na(hna.reshape(batch, SEQ, -1), p["na_bias"]).reshape(n, -1)
    yb = _gqa(qn.reshape(batch, SEQ, -1), kn, vb).reshape(n, -1)
    yc = _sgu(hsg, p["sg_w"], p["sg_b"], p["sg_ln_g"], p["sg_ln_b"], bd)
    x1, aff = _outproj(ya, yb, yc, x, p["w_out"], p["ln1_g"], p["ln1_b"], p["wr_hi"], p["wr_lo"])
    aff_t = aff.reshape(N_EXPERTS, n // LANES, LANES)
    sel = _route1(aff_t, cap)
    idx, gate, rho, off, cnt = _route2(aff_t, sel, cap)
    z = _ffn(idx.reshape(N_EXPERTS, cap), rho.reshape(N_EXPERTS, cap), x1, gate,
             p["w_gate"], p["w_up"], p["w_down"], cap)
    row_starts = jnp.concatenate(
        [off.reshape(-1)[::COMBINE_TT], jnp.full((1,), z.shape[0] // SLAB, f32)]).astype(i32)
    return _combine(row_starts, x1, off, cnt, p["ln2_g"], p["ln2_b"], z)


def _trunk(x, ln_in_g, ln_in_b, layers, cos_t, sin_t, bd):
    batch = x.shape[0]
    h = _ln_in(x.reshape(batch * SEQ, D_MODEL), ln_in_g, ln_in_b)
    for p in layers:
        h = _layer(h, batch, p, cos_t, sin_t, bd)
    return h.reshape(batch, SEQ, D_MODEL)


def _layer_params(l, w_in, na_rpb, q_norm, k_norm, sg_w, sg_b, sg_ln_g, sg_ln_b, w_out, ln1_g, ln1_b,
                  w_router, w_gate, w_up, w_down, ln2_g, ln2_b):
    wr = w_router[l]
    wr_hi = wr.astype(bf16)
    width = SG_GROUPS * HEAD_DIM
    return {
        "w_in": w_in[l].astype(bf16),
        "na_bias": _na_bias_table(na_rpb[l]),
        "q_norm": jnp.tile(q_norm[l], 2).reshape(1, LANES),
        "k_norm": jnp.tile(k_norm[l], 2).reshape(1, LANES),
        "sg_w": sg_w[l].astype(bf16),
        "sg_b": jnp.repeat(sg_b[l].T, HEAD_DIM, axis=1),
        "sg_ln_g": sg_ln_g[l].reshape(1, width),
        "sg_ln_b": sg_ln_b[l].reshape(1, width),
        "w_out": w_out[l].astype(bf16),
        "ln1_g": ln1_g[l].reshape(1, -1), "ln1_b": ln1_b[l].reshape(1, -1),
        "wr_hi": wr_hi.T,
        "wr_lo": jnp.concatenate([wr_hi, (wr - wr_hi.astype(f32)).astype(bf16)], axis=1).T,
        "w_gate": w_gate[l].astype(bf16), "w_up": w_up[l].astype(bf16), "w_down": w_down[l].astype(bf16),
        "ln2_g": ln2_g[l].reshape(1, -1), "ln2_b": ln2_b[l].reshape(1, -1),
    }


def kernel(x_prompt, x_sample, ln_in_g, ln_in_b, w_in, na_rpb, q_norm, k_norm, sg_w, sg_b, sg_ln_g, sg_ln_b,
           w_out, ln1_g, ln1_b, w_router, w_gate, w_up, w_down, ln2_g, ln2_b):
    layers = [_layer_params(l, w_in, na_rpb, q_norm, k_norm, sg_w, sg_b, sg_ln_g, sg_ln_b, w_out, ln1_g,
                            ln1_b, w_router, w_gate, w_up, w_down, ln2_g, ln2_b)
              for l in range(w_in.shape[0])]
    cos_t, sin_t = _rope_tables()
    bd = _seg_matrix()
    return (_trunk(x_prompt, ln_in_g, ln_in_b, layers, cos_t, sin_t, bd),
            _trunk(x_sample, ln_in_g, ln_in_b, layers, cos_t, sin_t, bd))
```

```python
import functools

import numpy as np
import jax
import jax.numpy as jnp
from jax import lax
from jax.experimental import pallas as pl
from jax.experimental.pallas import tpu as pltpu

f32 = jnp.float32
bf16 = jnp.bfloat16
i32 = jnp.int32

D_MODEL = 1024
SEQ = 4096
DEPTH = 4
GRID_W = 64
GRID_ROWS = SEQ // GRID_W
HEAD_DIM = 64
NA_HEADS = 4
NA_KH = 8
NA_KW = 16
GQA_HEADS = 8
GQA_KV = 2
ROPE_THETA = 10000.0
SG_GROUPS = 4
SG_CHUNK = 128
N_EXPERTS = 16
D_EXPERT = 2048
LN_EPS = 1e-5
QK_EPS = 1e-6
DN_ALPHA = (2 * DEPTH) ** 0.25
QK_SCALE = HEAD_DIM ** -0.5
LOG2E = 1.4426950408889634

LANES = 128
NEG = -1e30
MIB = 1024 * 1024

NA_ROWS_PER_STEP = 4
NA_BLOCKS_PER_STEP = 2
NA_WIN = 12
GQA_TQ = 512
GQA_TK = 512
ROW_TILE = 512
FFN_TM = 512
FFN_FCHUNK = 512
DMA_UNROLL = 16
ROUTE_PW = 512
COMBINE_TT = 256
COMBINE_RC = 256
COMBINE_STATIC = 3
SMEM_INDEX_BYTES = 128 * 1024


def _cp(semantics, vmem_mib=48):
    return pltpu.CompilerParams(dimension_semantics=semantics, vmem_limit_bytes=vmem_mib * MIB)


def _dot(a, b):
    return jnp.dot(a, b, preferred_element_type=f32)


def _dot_nt(a, b):
    return lax.dot_general(a, b, (((1,), (1,)), ((), ())), preferred_element_type=f32)


def _layer_norm(x, g, b):
    mu = jnp.mean(x, -1, keepdims=True)
    xc = x - mu
    var = jnp.mean(xc * xc, -1, keepdims=True)
    return xc * lax.rsqrt(var + LN_EPS) * g + b


def _split2(x):
    hi = x.astype(bf16)
    lo = (x - hi.astype(f32)).astype(bf16)
    return hi, lo


SLAB = D_MODEL // LANES


def _load_slabs(ref, rows):
    return jnp.concatenate([ref[pl.ds(a, rows, stride=SLAB), :] for a in range(SLAB)], axis=1)


def _store_slabs(ref, x):
    rows = x.shape[0]
    for a in range(SLAB):
        ref[pl.ds(a, rows, stride=SLAB), :] = x[:, a * LANES:(a + 1) * LANES]


def _seg_mean(x, bd):
    hi, lo = _split2(x)
    return _dot(hi, bd) + _dot(lo, bd)


def _ln_in_kernel(x_ref, g_ref, b_ref, o_ref):
    o_ref[...] = _layer_norm(x_ref[...], g_ref[...], b_ref[...])


def _ln_in(x, g, b):
    n = x.shape[0]
    tm = 1024
    row = pl.BlockSpec((tm, D_MODEL), lambda i: (i, 0))
    vec = pl.BlockSpec((1, D_MODEL), lambda i: (0, 0))
    return pl.pallas_call(
        _ln_in_kernel, name="ln_in", grid=(n // tm,), in_specs=[row, vec, vec], out_specs=row,
        out_shape=jax.ShapeDtypeStruct((n, D_MODEL), f32), compiler_params=_cp(("parallel",)),
    )(x, g.reshape(1, -1), b.reshape(1, -1))


def _proj_kernel(x_ref, w_ref, cos_ref, sin_ref, qg_ref, kg_ref, bd_ref,
                 hna_ref, qn_ref, kn_ref, vb_ref, hsg_ref):
    tm = x_ref.shape[0]
    xb = x_ref[...].astype(bf16)

    wide = {}

    def mm(c):
        if c // 2 not in wide:
            wide[c // 2] = _dot(xb, w_ref[:, 512 * (c // 2):512 * (c // 2 + 1)])
        return wide[c // 2][:, 256 * (c % 2):256 * (c % 2 + 1)]

    hna_ref[:, 0:256] = (mm(0) * QK_SCALE).astype(bf16)
    hna_ref[:, 256:512] = mm(1).astype(bf16)
    hna_ref[:, 512:768] = mm(2).astype(bf16)

    cos = cos_ref[...]
    sin = sin_ref[...]
    bd = bd_ref[...]
    even = (lax.broadcasted_iota(i32, (tm, LANES), 1) & 1) == 0

    def norm_rope(x, g):
        y = x * lax.rsqrt(_seg_mean(x * x, bd) + QK_EPS) * g
        partner = jnp.where(even, pltpu.roll(y, LANES - 1, 1), pltpu.roll(y, 1, 1))
        return y * cos + partner * sin

    qg = qg_ref[...]
    for c in (3, 4):
        a = mm(c)
        for j in range(2):
            col = (c - 3) * 256 + j * LANES
            qn_ref[:, col:col + LANES] = (
                norm_rope(a[:, j * LANES:(j + 1) * LANES], qg) * (QK_SCALE * LOG2E)).astype(bf16)
    a = mm(5)
    kn = norm_rope(a[:, :LANES], kg_ref[...]).astype(bf16)
    kn_ref[0, 0] = kn[:, :HEAD_DIM]
    kn_ref[0, 1] = kn[:, HEAD_DIM:]
    vb = a[:, LANES:].astype(bf16)
    ones = jnp.ones((tm, HEAD_DIM), bf16)
    vb_ref[0, 0] = jnp.concatenate([vb[:, :HEAD_DIM], ones], axis=1)
    vb_ref[0, 1] = jnp.concatenate([vb[:, HEAD_DIM:], ones], axis=1)
    hsg_ref[:, :256] = mm(6)
    hsg_ref[:, 256:] = mm(7)


def _proj(x, w, cos_t, sin_t, qg, kg, bd, batch):
    n = x.shape[0]
    tm = ROW_TILE
    per_seq = SEQ // tm
    row = lambda width: pl.BlockSpec((tm, width), lambda i: (i, 0))
    const = lambda shape: pl.BlockSpec(shape, lambda i: tuple(0 for _ in shape))
    tab = pl.BlockSpec((tm, LANES), lambda i: (i % per_seq, 0))
    head_major = lambda width: pl.BlockSpec((1, GQA_KV, tm, width),
                                            lambda i: (i // per_seq, 0, i % per_seq, 0))
    return pl.pallas_call(
        _proj_kernel, name="proj", grid=(n // tm,),
        in_specs=[row(D_MODEL), const(w.shape), tab, tab, const((1, LANES)), const((1, LANES)),
                  const((LANES, LANES))],
        out_specs=[row(768), row(512), head_major(HEAD_DIM), head_major(2 * HEAD_DIM), row(512)],
        out_shape=[jax.ShapeDtypeStruct((n, 768), bf16),
                   jax.ShapeDtypeStruct((n, 512), bf16),
                   jax.ShapeDtypeStruct((batch, GQA_KV, SEQ, HEAD_DIM), bf16),
                   jax.ShapeDtypeStruct((batch, GQA_KV, SEQ, 2 * HEAD_DIM), bf16),
                   jax.ShapeDtypeStruct((n, 512), f32)],
        compiler_params=_cp(("parallel",)),
    )(x, w, cos_t, sin_t, qg, kg, bd)


def _na_window_start(r0):
    return jnp.clip(r0 - NA_KH // 2, 0, GRID_ROWS - NA_WIN)


def _na_kernel(q_ref, k_ref, v_ref, *rest):
    bias_refs, o_ref = rest[:-1], rest[-1]
    tq = NA_ROWS_PER_STEP * GRID_W
    for blk, bias_ref in enumerate(bias_refs):
        r0 = (pl.program_id(1) * NA_BLOCKS_PER_STEP + blk) * NA_ROWS_PER_STEP
        krow = pl.multiple_of(_na_window_start(r0) * GRID_W, GRID_W)
        q = q_ref[0, blk * tq:(blk + 1) * tq, :]
        kw = k_ref[0, pl.ds(krow, NA_WIN * GRID_W), :]
        vw = v_ref[0, pl.ds(krow, NA_WIN * GRID_W), :]
        outs = []
        for h in range(NA_HEADS):
            cs = slice(h * HEAD_DIM, (h + 1) * HEAD_DIM)
            s = _dot_nt(q[:, cs], kw[:, cs]) + bias_ref[0, h]
            m = jnp.max(s, -1, keepdims=True)
            e = jnp.exp(s - m)
            l = jnp.sum(e, -1, keepdims=True)
            outs.append(_dot(e.astype(bf16), vw[:, cs]) / l)
        o_ref[0, blk * tq:(blk + 1) * tq, :] = jnp.concatenate(outs, -1).astype(bf16)


def _na_pattern(block):
    last = GRID_ROWS // NA_ROWS_PER_STEP - 1
    return jnp.where(block == 0, 0, jnp.where(block == last, 2, 1))


def _na(hna, bias_tab):
    batch = hna.shape[0]
    nb = NA_BLOCKS_PER_STEP
    tq = nb * NA_ROWS_PER_STEP * GRID_W
    width = NA_HEADS * HEAD_DIM
    bias_spec = lambda blk: pl.BlockSpec((1,) + bias_tab.shape[1:],
                                         lambda b, i: (_na_pattern(i * nb + blk), 0, 0, 0))
    return pl.pallas_call(
        _na_kernel, name="na", grid=(batch, SEQ // tq),
        in_specs=[pl.BlockSpec((1, tq, width), lambda b, i: (b, i, 0)),
                  pl.BlockSpec((1, SEQ, width), lambda b, i: (b, 0, 1)),
                  pl.BlockSpec((1, SEQ, width), lambda b, i: (b, 0, 2))]
                 + [bias_spec(blk) for blk in range(nb)],
        out_specs=pl.BlockSpec((1, tq, width), lambda b, i: (b, i, 0)),
        out_shape=jax.ShapeDtypeStruct((batch, SEQ, width), bf16),
        compiler_params=_cp(("parallel", "arbitrary")),
    )(hna, hna, hna, *([bias_tab] * nb))


def _na_bias_table(rpb):
    col = np.arange(GRID_W)
    col_start = np.clip(col - NA_KW // 2, 0, GRID_W - NA_KW)
    kc = np.arange(GRID_W)
    col_ok = (kc[None, :] >= col_start[:, None]) & (kc[None, :] < col_start[:, None] + NA_KW)
    dcol = np.clip(kc[None, :] - col[:, None] + NA_KW - 1, 0, 2 * NA_KW - 2)
    pick = jnp.asarray(dcol[None] == np.arange(2 * NA_KW - 1)[:, None, None], f32)
    tabs = []
    for r0 in (0, NA_ROWS_PER_STEP, GRID_ROWS - NA_ROWS_PER_STEP):
        ws = int(np.clip(r0 - NA_KH // 2, 0, GRID_ROWS - NA_WIN))
        r = r0 + np.arange(NA_ROWS_PER_STEP)
        rs = np.clip(r - NA_KH // 2, 0, GRID_ROWS - NA_KH)
        krow = ws + np.arange(NA_WIN)
        row_ok = (krow[None, :] >= rs[:, None]) & (krow[None, :] < rs[:, None] + NA_KH)
        drow = np.clip(krow[None, :] - r[:, None] + NA_KH - 1, 0, 2 * NA_KH - 2)
        g = jnp.einsum("hawk,kcd->hacwd", rpb[:, drow], pick, precision=lax.Precision.HIGHEST)
        ok = row_ok[:, None, :, None] & col_ok[None, :, None, :]
        tabs.append(jnp.where(ok[None], g, NEG).reshape(
            NA_HEADS, NA_ROWS_PER_STEP * GRID_W, NA_WIN * GRID_W))
    return jnp.stack(tabs).astype(f32)


def _gqa_kernel(q_ref, k_ref, v_ref, o_ref):
    g_per_kv = GQA_HEADS // GQA_KV
    q = q_ref[0]
    q4 = jnp.concatenate([q[:, g * HEAD_DIM:(g + 1) * HEAD_DIM] for g in range(g_per_kv)], axis=0)
    rows = g_per_kv * GQA_TQ

    m = jnp.full((rows, 1), NEG, f32)
    acc = jnp.zeros((rows, 2 * HEAD_DIM), f32)
    for j in range(SEQ // GQA_TK):
        kc = k_ref[0, 0, j * GQA_TK:(j + 1) * GQA_TK, :]
        vc = v_ref[0, 0, j * GQA_TK:(j + 1) * GQA_TK, :]
        s = _dot_nt(q4, kc)
        mn = jnp.maximum(m, jnp.max(s, -1, keepdims=True))
        acc = jnp.exp2(m - mn) * acc + _dot(jnp.exp2(s - mn).astype(bf16), vc)
        m = mn
    o = acc[:, :HEAD_DIM] / acc[:, HEAD_DIM:HEAD_DIM + 1]
    for g in range(g_per_kv):
        o_ref[0, :, g * HEAD_DIM:(g + 1) * HEAD_DIM] = o[g * GQA_TQ:(g + 1) * GQA_TQ].astype(bf16)


def _gqa(qn, kn, vb):
    batch = qn.shape[0]
    width = (GQA_HEADS // GQA_KV) * HEAD_DIM
    qspec = pl.BlockSpec((1, GQA_TQ, width), lambda b, kv, i: (b, i, kv))
    kvspec = lambda width: pl.BlockSpec((1, 1, SEQ, width), lambda b, kv, i: (b, kv, 0, 0))
    return pl.pallas_call(
        _gqa_kernel, name="gqa", grid=(batch, GQA_KV, SEQ // GQA_TQ),
        in_specs=[qspec, kvspec(HEAD_DIM), kvspec(2 * HEAD_DIM)], out_specs=qspec,
        out_shape=jax.ShapeDtypeStruct((batch, SEQ, GQA_HEADS * HEAD_DIM), bf16),
        compiler_params=_cp(("parallel", "parallel", "arbitrary")),
    )(qn, kn, vb)


def _sgu_kernel(h_ref, ws_ref, bs_ref, g_ref, b_ref, bd_ref, o_ref):
    tm = h_ref.shape[0]
    width = SG_GROUPS * HEAD_DIM
    bd = bd_ref[...]
    u = jax.nn.gelu(h_ref[:, :width])
    v = jax.nn.gelu(h_ref[:, width:])
    parts = []
    for j in range(width // LANES):
        x = v[:, j * LANES:(j + 1) * LANES]
        xc = x - _seg_mean(x, bd)
        parts.append(xc * lax.rsqrt(_seg_mean(xc * xc, bd) + LN_EPS))
    vn = (jnp.concatenate(parts, 1) * g_ref[...] + b_ref[...]).astype(bf16)
    bs = bs_ref[...]
    for c in range(tm // SG_CHUNK):
        rows = slice(c * SG_CHUNK, (c + 1) * SG_CHUNK)
        mixed = jnp.concatenate(
            [_dot(ws_ref[g], vn[rows, g * HEAD_DIM:(g + 1) * HEAD_DIM]) for g in range(SG_GROUPS)], axis=1)
        o_ref[rows, :] = (u[rows] * (mixed + bs)).astype(bf16)


def _sgu(hsg, ws, bs_tab, g, b, bd):
    n = hsg.shape[0]
    tm = ROW_TILE
    width = SG_GROUPS * HEAD_DIM
    const = lambda shape: pl.BlockSpec(shape, lambda i: tuple(0 for _ in shape))
    return pl.pallas_call(
        _sgu_kernel, name="sgu", grid=(n // tm,),
        in_specs=[pl.BlockSpec((tm, 2 * width), lambda i: (i, 0)), const(ws.shape), const(bs_tab.shape),
                  const((1, width)), const((1, width)), const((LANES, LANES))],
        out_specs=pl.BlockSpec((tm, width), lambda i: (i, 0)),
        out_shape=jax.ShapeDtypeStruct((n, width), bf16),
        compiler_params=_cp(("parallel",)),
    )(hsg, ws, bs_tab, g, b, bd)


def _outproj_kernel(ya_ref, yb_ref, yc_ref, x_ref, w_ref, g_ref, b_ref, wrh_ref, wrl_ref, x1_ref, aff_ref):
    mix = (_dot(ya_ref[...], w_ref[0:256]) + _dot(yb_ref[...], w_ref[256:768])
           + _dot(yc_ref[...], w_ref[768:1024]))
    x1 = _layer_norm(DN_ALPHA * x_ref[...] + mix, g_ref[...], b_ref[...])
    _store_slabs(x1_ref, x1)
    hi, lo = _split2(x1)
    both = _dot_nt(wrl_ref[...], hi)
    logits = both[:N_EXPERTS] + both[N_EXPERTS:] + _dot_nt(wrh_ref[...], lo)
    m = jnp.max(logits, 0, keepdims=True)
    e = jnp.exp(logits - m)
    aff_ref[...] = e / jnp.sum(e, 0, keepdims=True)


def _outproj(ya, yb, yc, x, w, g, b, wrh, wrl):
    n = x.shape[0]
    tm = ROW_TILE
    row = lambda width: pl.BlockSpec((tm, width), lambda i: (i, 0))
    const = lambda shape: pl.BlockSpec(shape, lambda i: tuple(0 for _ in shape))
    return pl.pallas_call(
        _outproj_kernel, name="outproj", grid=(n // tm,),
        in_specs=[row(256), row(512), row(256), row(D_MODEL), const(w.shape), const((1, D_MODEL)),
                  const((1, D_MODEL)), const(wrh.shape), const(wrl.shape)],
        out_specs=[pl.BlockSpec((tm * SLAB, LANES), lambda i: (i, 0)),
                   pl.BlockSpec((N_EXPERTS, tm), lambda i: (0, i))],
        out_shape=[jax.ShapeDtypeStruct((n * SLAB, LANES), f32), jax.ShapeDtypeStruct((N_EXPERTS, n), f32)],
        compiler_params=_cp(("parallel",)),
    )(ya, yb, yc, x, w, g, b, wrh, wrl)


def _tri_incl():
    r = lax.broadcasted_iota(i32, (LANES, LANES), 0)
    c = lax.broadcasted_iota(i32, (LANES, LANES), 1)
    return (r <= c).astype(bf16)


def _tri_strict_lower(nc):
    r = lax.broadcasted_iota(i32, (nc, nc), 0)
    c = lax.broadcasted_iota(i32, (nc, nc), 1)
    return (c < r).astype(bf16)


def _byte_planes(x, planes):
    xi = x.astype(i32)
    return [((xi >> (8 * k)) & 255).astype(f32).astype(bf16) for k in range(planes)]


def _token_prefix(m, planes):
    nc = m.shape[0]
    lincl = _dot(m.astype(bf16), _tri_incl())
    tot = jnp.broadcast_to(lincl[:, LANES - 1:LANES], m.shape)
    sl = _tri_strict_lower(nc)
    pref = sum(_dot(sl, p) * float(256 ** k) for k, p in enumerate(_byte_planes(tot, planes)))
    return lincl, tot, pref


def _route1_kernel(aff_ref, sel_ref, *, cap):
    bits = lax.bitcast_convert_type(aff_ref[0], i32)

    def enough(cand):
        return jnp.sum((bits >= cand).astype(f32)) >= cap

    def body(i, t):
        one = lax.shift_left(jnp.int32(1), 28 - 2 * i)
        c1, c2, c3 = t | one, t | (one << 1), t | (one << 1) | one
        return jnp.where(enough(c3), c3, jnp.where(enough(c2), c2, jnp.where(enough(c1), c1, t)))

    top = jnp.int32(1 << 30)
    thr = lax.fori_loop(0, 15, body, jnp.where(enough(top), top, 0))
    gt = bits > thr
    eq = bits == thr
    need = cap - jnp.sum(gt.astype(f32))
    eqf = eq.astype(f32)
    lincl, _, pref = _token_prefix(eqf, 1)
    before = pref + lincl - eqf
    sel_ref[0] = (gt | (eq & (before < need))).astype(f32)


def _route1(aff_t, cap):
    ne, nc, _ = aff_t.shape
    spec = pl.BlockSpec((1, nc, LANES), lambda e: (e, 0, 0))
    return pl.pallas_call(
        functools.partial(_route1_kernel, cap=cap), name="route1", grid=(ne,), in_specs=[spec],
        out_specs=spec,
        out_shape=jax.ShapeDtypeStruct(aff_t.shape, f32), compiler_params=_cp(("parallel",)),
    )(aff_t)


def _route2_kernel(aff_ref, sel_ref, idx_ref, gate_ref, rho_ref, off_ref, cnt_ref, *, cap):
    e = pl.program_id(0)
    nc = aff_ref.shape[1]
    pw = ROUTE_PW

    count = jnp.zeros((nc, LANES), f32)
    rank = jnp.zeros((nc, LANES), f32)
    for k in range(N_EXPERTS):
        sk = sel_ref[k]
        count = count + sk
        rank = rank + jnp.where(k < e, sk, 0.0)
    lc, _, pc = _token_prefix(count, 2)
    off = pc + lc - count
    off_ref[...] = off
    cnt_ref[...] = count

    sel = sel_ref[e]
    lincl, tot, pref = _token_prefix(sel, 1)
    cse = pref[:, 0:1]
    csi = cse + tot[:, 0:1]
    chunk_id = lax.broadcasted_iota(i32, (nc, 1), 0).astype(f32)

    lt = lincl.T.astype(bf16)
    a = aff_ref[0].T
    a1 = a.astype(bf16)
    r1 = a - a1.astype(f32)
    a2 = r1.astype(bf16)
    a3 = (r1 - a2.astype(f32)).astype(bf16)
    dest = _byte_planes((off + rank).T, 3)
    tables = jnp.concatenate([lt, a1, a2, a3] + dest, axis=0)
    lane_id = lax.broadcasted_iota(i32, (LANES, pw), 0).astype(f32)

    def tile(k, carry):
        p = (k * pw + lax.broadcasted_iota(i32, (1, pw), 1)).astype(f32)
        oh = (cse <= p) & (p < csi)
        ohb = oh.astype(f32).astype(bf16)
        chunk = jnp.sum(jnp.where(oh, chunk_id, 0.0), axis=0, keepdims=True)
        local = p - jnp.sum(jnp.where(oh, cse, 0.0), axis=0, keepdims=True)
        looked = _dot(tables, ohb)
        part = lambda t: looked[t * LANES:(t + 1) * LANES]
        jstar = jnp.sum((part(0) <= local).astype(f32), axis=0, keepdims=True)
        hit = lane_id == jstar
        g = part(1) + part(2) + part(3)
        r = part(4) + 256.0 * part(5) + 65536.0 * part(6)
        idx_ref[0, pl.ds(k, 1), :] = (chunk * LANES + jstar).astype(i32)
        gate_ref[0, pl.ds(k, 1), :] = jnp.sum(jnp.where(hit, g, 0.0), axis=0, keepdims=True)
        rho_ref[0, pl.ds(k, 1), :] = jnp.sum(jnp.where(hit, r, 0.0), axis=0, keepdims=True).astype(i32)
        return carry

    lax.fori_loop(0, cap // pw, tile, 0)


def _route2(aff_t, sel, cap):
    ne, nc, _ = aff_t.shape
    pw = ROUTE_PW
    slot = pl.BlockSpec((1, cap // pw, pw), lambda e: (e, 0, 0))
    tok = pl.BlockSpec((nc, LANES), lambda e: (0, 0))
    slot_shape = lambda dt: jax.ShapeDtypeStruct((ne, cap // pw, pw), dt)
    return pl.pallas_call(
        functools.partial(_route2_kernel, cap=cap), name="route2", grid=(ne,),
        in_specs=[pl.BlockSpec((1, nc, LANES), lambda e: (e, 0, 0)),
                  pl.BlockSpec((ne, nc, LANES), lambda e: (0, 0, 0))],
        out_specs=[slot, slot, slot, tok, tok],
        out_shape=[slot_shape(i32), slot_shape(f32), slot_shape(i32),
                   jax.ShapeDtypeStruct((nc, LANES), f32), jax.ShapeDtypeStruct((nc, LANES), f32)],
        compiler_params=_cp(("arbitrary",)),
    )(aff_t, sel)


def _row_to_col(row):
    n = row.shape[1]
    r = lax.broadcasted_iota(i32, (n, n), 0)
    c = lax.broadcasted_iota(i32, (n, n), 1)
    return jnp.sum(jnp.where(r == c, row, 0.0), axis=1, keepdims=True)


def _ffn_kernel(idx_sm, rho_sm, x_hbm, gate_ref, wg_ref, wu_ref, wd_ref, *rest, aliased):
    if aliased:
        _, z_hbm, xbuf, ybuf, gsem, ssem = rest
    else:
        z_hbm, xbuf, ybuf, gsem, ssem = rest
    tm = FFN_TM
    mt = pl.num_programs(1)
    total = pl.num_programs(0) * mt
    s = pl.program_id(0) * mt + pl.program_id(1)
    slot = s % 2

    def gather_copy(tok, i, sl):
        return pltpu.make_async_copy(x_hbm.at[pl.ds(pl.multiple_of(tok * SLAB, SLAB), SLAB), :],
                                     xbuf.at[sl, pl.ds(pl.multiple_of(i * SLAB, SLAB), SLAB), :], gsem.at[sl])

    def scatter_copy(row, i, sl):
        return pltpu.make_async_copy(ybuf.at[sl, pl.ds(pl.multiple_of(i * SLAB, SLAB), SLAB), :],
                                     z_hbm.at[pl.ds(pl.multiple_of(row * SLAB, SLAB), SLAB), :], ssem.at[sl])

    def start_gather(step, sl):
        def body(i, c):
            gather_copy(idx_sm[step * tm + i], i, sl).start()
            return c
        lax.fori_loop(0, tm, body, 0, unroll=DMA_UNROLL)

    def wait_gather(sl):
        pltpu.make_async_copy(x_hbm.at[pl.ds(0, tm * SLAB), :], xbuf.at[sl], gsem.at[sl]).wait()

    def wait_scatter(sl):
        pltpu.make_async_copy(ybuf.at[sl], z_hbm.at[pl.ds(0, tm * SLAB), :], ssem.at[sl]).wait()

    @pl.when(s == 0)
    def _():
        start_gather(0, 0)

    @pl.when(s + 1 < total)
    def _():
        start_gather(s + 1, 1 - slot)

    wait_gather(slot)

    @pl.when(s >= 2)
    def _():
        wait_scatter(slot)

    x = _load_slabs(xbuf.at[slot], tm).astype(bf16)
    y = jnp.zeros((tm, D_MODEL), f32)
    for c in range(D_EXPERT // FFN_FCHUNK):
        cols = slice(c * FFN_FCHUNK, (c + 1) * FFN_FCHUNK)
        hg = _dot(x, wg_ref[0, 0, :, cols])
        hu = _dot(x, wu_ref[0, 0, :, cols])
        y = y + _dot((jax.nn.silu(hg) * hu).astype(bf16), wd_ref[0, 0, cols, :])
    _store_slabs(ybuf.at[slot], y * _row_to_col(gate_ref[0]))

    def start_scatter(i, c):
        scatter_copy(rho_sm[s * tm + i], i, slot).start()
        return c
    lax.fori_loop(0, tm, start_scatter, 0, unroll=DMA_UNROLL)

    @pl.when(s == total - 1)
    def _():
        wait_scatter(slot)

        @pl.when(total >= 2)
        def _():
            wait_scatter(1 - slot)


def _ffn_call(idx, rho, x1, gate, wg, wu, wd, layer, z, e0, eg, cap):
    tm = FFN_TM
    mt = cap // tm
    n_rows = N_EXPERTS * cap
    aliased = z is not None
    any_spec = pl.BlockSpec(memory_space=pl.ANY)
    in_specs = [any_spec,
                pl.BlockSpec((1, 1, tm), lambda e, m, *_: ((e0 + e) * mt + m, 0, 0)),
                pl.BlockSpec((1, 1, D_MODEL, D_EXPERT), lambda e, m, *_: (layer, e0 + e, 0, 0)),
                pl.BlockSpec((1, 1, D_MODEL, D_EXPERT), lambda e, m, *_: (layer, e0 + e, 0, 0)),
                pl.BlockSpec((1, 1, D_EXPERT, D_MODEL), lambda e, m, *_: (layer, e0 + e, 0, 0))]
    args = [x1, gate, wg, wu, wd]
    if aliased:
        in_specs.append(any_spec)
        args.append(z)
    grid_spec = pltpu.PrefetchScalarGridSpec(
        num_scalar_prefetch=2, grid=(eg, mt), in_specs=in_specs, out_specs=any_spec,
        scratch_shapes=[pltpu.VMEM((2, tm * SLAB, LANES), f32), pltpu.VMEM((2, tm * SLAB, LANES), f32),
                        pltpu.SemaphoreType.DMA((2,)), pltpu.SemaphoreType.DMA((2,))])
    return pl.pallas_call(
        functools.partial(_ffn_kernel, aliased=aliased), name="ffn", grid_spec=grid_spec,
        out_shape=jax.ShapeDtypeStruct((n_rows * SLAB, LANES), f32),
        input_output_aliases={7: 0} if aliased else {},
        compiler_params=pltpu.CompilerParams(dimension_semantics=("arbitrary", "arbitrary"),
                                             vmem_limit_bytes=56 * MIB, has_side_effects=True),
    )(idx[e0:e0 + eg].reshape(-1), rho[e0:e0 + eg].reshape(-1), *args)


def _ffn(idx, rho, x1, gate, wg, wu, wd, layer, cap):
    eg = max(1, min(N_EXPERTS, SMEM_INDEX_BYTES // (2 * 4 * cap)))
    gate3 = gate.reshape(N_EXPERTS * cap // FFN_TM, 1, FFN_TM)
    z = None
    for e0 in range(0, N_EXPERTS, eg):
        z = _ffn_call(idx, rho, x1, gate3, wg, wu, wd, layer, z, e0, eg, cap)
    return z


def _combine_kernel(rs_sm, x_ref, off_ref, cnt_ref, g_ref, b_ref, z_hbm, o_ref, zbuf, sem, *, total_rows):
    tt = COMBINE_TT
    rc = COMBINE_RC
    i = pl.program_id(0)

    def tile_rows(t):
        first = (rs_sm[t] >> 3) << 3
        return first, (rs_sm[t + 1] - first + rc - 1) >> (rc.bit_length() - 1)

    r0, nch = tile_rows(i)

    offs = []
    ends = []
    for c in range(tt // LANES):
        o_row = off_ref[0, c:c + 1, :]
        offs.append(_row_to_col(o_row))
        ends.append(_row_to_col(o_row + cnt_ref[0, c:c + 1, :]))
    off_col = jnp.concatenate(offs, 0)
    end_col = jnp.concatenate(ends, 0)

    def chunk_start(first, j):
        return pl.multiple_of(jnp.minimum(first + j * rc, total_rows - rc), 8)

    def chunk_copy(first, j, sl):
        return pltpu.make_async_copy(z_hbm.at[pl.ds(chunk_start(first, j) * SLAB, rc * SLAB), :], zbuf.at[sl],
                                     sem.at[sl])

    def contribution(j, sl):
        rowid = chunk_start(r0, j) + lax.broadcasted_iota(i32, (1, rc), 1)
        fresh = rowid >= r0 + j * rc
        rowf = rowid.astype(f32)
        seg = ((off_col <= rowf) & (rowf < end_col) & fresh).astype(f32).astype(bf16)
        hi, lo = _split2(_load_slabs(zbuf.at[sl], rc))
        return _dot(seg, hi) + _dot(seg, lo)

    last = pl.num_programs(0) - 1
    nxt, _ = tile_rows(jnp.minimum(i + 1, last))

    cur = (i % 2) * COMBINE_STATIC
    oth = COMBINE_STATIC - cur

    @pl.when(i == 0)
    def _():
        for j in range(COMBINE_STATIC):
            chunk_copy(r0, j, cur + j).start()

    for j in range(COMBINE_STATIC):
        chunk_copy(r0, j, cur + j).wait()
    for j in range(COMBINE_STATIC):
        chunk_copy(nxt, j, oth + j).start()

    acc = jnp.zeros((tt, D_MODEL), f32)
    for j in range(COMBINE_STATIC):
        acc = acc + contribution(j, cur + j)

    def tail(j, acc):
        cp = chunk_copy(r0, j, 2 * COMBINE_STATIC)
        cp.start()
        cp.wait()
        return acc + contribution(j, 2 * COMBINE_STATIC)

    acc = lax.fori_loop(COMBINE_STATIC, jnp.maximum(nch, COMBINE_STATIC), tail, acc)

    @pl.when(i == last)
    def _():
        for j in range(COMBINE_STATIC):
            chunk_copy(nxt, j, oth + j).wait()

    o_ref[...] = _layer_norm(DN_ALPHA * _load_slabs(x_ref, tt) + acc, g_ref[...], b_ref[...])


def _combine(row_starts, x1, off, cnt, g, b, z):
    n = x1.shape[0] // SLAB
    tt = COMBINE_TT
    per = tt // LANES
    tok = pl.BlockSpec((1, per, LANES), lambda i, *_: (i, 0, 0))
    vec = pl.BlockSpec((1, D_MODEL), lambda i, *_: (0, 0))
    row = pl.BlockSpec((tt, D_MODEL), lambda i, *_: (i, 0))
    grid_spec = pltpu.PrefetchScalarGridSpec(
        num_scalar_prefetch=1, grid=(n // tt,),
        in_specs=[pl.BlockSpec((tt * SLAB, LANES), lambda i, *_: (i, 0)), tok, tok, vec, vec,
                  pl.BlockSpec(memory_space=pl.ANY)], out_specs=row,
        scratch_shapes=[pltpu.VMEM((2 * COMBINE_STATIC + 1, COMBINE_RC * SLAB, LANES), f32),
                        pltpu.SemaphoreType.DMA((2 * COMBINE_STATIC + 1,))])
    return pl.pallas_call(
        functools.partial(_combine_kernel, total_rows=z.shape[0] // SLAB), name="combine",
        grid_spec=grid_spec,
        out_shape=jax.ShapeDtypeStruct((n, D_MODEL), f32), compiler_params=_cp(("arbitrary",)),
    )(row_starts, x1, off.reshape(n // tt, per, LANES), cnt.reshape(n // tt, per, LANES), g, b, z)


def _rope_tables():
    t = np.arange(SEQ)
    row = (t // GRID_W).astype(np.float32)
    col = (t % GRID_W).astype(np.float32)
    n_freq = HEAD_DIM // 4
    inv = jnp.asarray(ROPE_THETA, f32) ** (-jnp.arange(n_freq, dtype=f32) / n_freq)
    ang = jnp.concatenate([row[:, None] * inv, col[:, None] * inv], -1)
    cos = jnp.repeat(jnp.cos(ang), 2, axis=1)
    sin = jnp.repeat(jnp.sin(ang), 2, axis=1)
    sign = jnp.tile(jnp.asarray([-1.0, 1.0], f32), HEAD_DIM // 2)
    return jnp.tile(cos, (1, 2)), jnp.tile(sin * sign, (1, 2))


def _seg_matrix():
    r = np.arange(LANES)
    return jnp.asarray((r[:, None] // HEAD_DIM == r[None, :] // HEAD_DIM) / HEAD_DIM, bf16)


def _layer(x, batch, p, cos_t, sin_t, bd):
    n = x.shape[0]
    cap = 2 * n // N_EXPERTS
    hna, qn, kn, vb, hsg = _proj(x, p["w_in"], cos_t, sin_t, p["q_norm"], p["k_norm"], bd, batch)
    ya = _na(hna.reshape(batch, SEQ, -1), p["na_bias"]).reshape(n, -1)
    yb = _gqa(qn.reshape(batch, SEQ, -1), kn, vb).reshape(n, -1)
    yc = _sgu(hsg, p["sg_w"], p["sg_b"], p["sg_ln_g"], p["sg_ln_b"], bd)
    x1, aff = _outproj(ya, yb, yc, x, p["w_out"], p["ln1_g"], p["ln1_b"], p["wr_hi"], p["wr_lo"])
    aff_t = aff.reshape(N_EXPERTS, n // LANES, LANES)
    sel = _route1(aff_t, cap)
    idx, gate, rho, off, cnt = _route2(aff_t, sel, cap)
    z = _ffn(idx.reshape(N_EXPERTS, cap), rho.reshape(N_EXPERTS, cap), x1, gate,
             p["w_gate"], p["w_up"], p["w_down"], p["layer"], cap)
    row_starts = jnp.concatenate(
        [off.reshape(-1)[::COMBINE_TT], jnp.full((1,), z.shape[0] // SLAB, f32)]).astype(i32)
    return _combine(row_starts, x1, off, cnt, p["ln2_g"], p["ln2_b"], z)


def _trunk(x, ln_in_g, ln_in_b, layers, cos_t, sin_t, bd):
    batch = x.shape[0]
    h = _ln_in(x.reshape(batch * SEQ, D_MODEL), ln_in_g, ln_in_b)
    for p in layers:
        h = _layer(h, batch, p, cos_t, sin_t, bd)
    return h.reshape(batch, SEQ, D_MODEL)


def _layer_params(l, w_in, na_rpb, q_norm, k_norm, sg_w, sg_b, sg_ln_g, sg_ln_b, w_out, ln1_g, ln1_b,
                  w_router, w_gate, w_up, w_down, ln2_g, ln2_b):
    wr = w_router[l]
    wr_hi = wr.astype(bf16)
    width = SG_GROUPS * HEAD_DIM
    return {
        "w_in": w_in[l].astype(bf16),
        "na_bias": _na_bias_table(na_rpb[l]),
        "q_norm": jnp.tile(q_norm[l], 2).reshape(1, LANES),
        "k_norm": jnp.tile(k_norm[l], 2).reshape(1, LANES),
        "sg_w": sg_w[l].astype(bf16),
        "sg_b": jnp.repeat(sg_b[l].T, HEAD_DIM, axis=1),
        "sg_ln_g": sg_ln_g[l].reshape(1, width),
        "sg_ln_b": sg_ln_b[l].reshape(1, width),
        "w_out": w_out[l].astype(bf16),
        "ln1_g": ln1_g[l].reshape(1, -1), "ln1_b": ln1_b[l].reshape(1, -1),
        "wr_hi": wr_hi.T,
        "wr_lo": jnp.concatenate([wr_hi, (wr - wr_hi.astype(f32)).astype(bf16)], axis=1).T,
        "w_gate": w_gate, "w_up": w_up, "w_down": w_down, "layer": l,
        "ln2_g": ln2_g[l].reshape(1, -1), "ln2_b": ln2_b[l].reshape(1, -1),
    }


def kernel(x_prompt, x_sample, ln_in_g, ln_in_b, w_in, na_rpb, q_norm, k_norm, sg_w, sg_b, sg_ln_g, sg_ln_b,
           w_out, ln1_g, ln1_b, w_router, w_gate, w_up, w_down, ln2_g, ln2_b):
    w_gate, w_up, w_down = (w.astype(bf16) for w in (w_gate, w_up, w_down))
    layers = [_layer_params(l, w_in, na_rpb, q_norm, k_norm, sg_w, sg_b, sg_ln_g, sg_ln_b, w_out, ln1_g,
                            ln1_b, w_router, w_gate, w_up, w_down, ln2_g, ln2_b)
              for l in range(w_in.shape[0])]
    cos_t, sin_t = _rope_tables()
    bd = _seg_matrix()
    return (_trunk(x_prompt, ln_in_g, ln_in_b, layers, cos_t, sin_t, bd),
            _trunk(x_sample, ln_in_g, ln_in_b, layers, cos_t, sin_t, bd))
```

```python
import functools

import numpy as np
import jax
import jax.numpy as jnp
from jax import lax
from jax.experimental import pallas as pl
from jax.experimental.pallas import tpu as pltpu

f32 = jnp.float32
bf16 = jnp.bfloat16
i32 = jnp.int32

D_MODEL = 1024
SEQ = 4096
DEPTH = 4
GRID_W = 64
GRID_ROWS = SEQ // GRID_W
HEAD_DIM = 64
NA_HEADS = 4
NA_KH = 8
NA_KW = 16
GQA_HEADS = 8
GQA_KV = 2
ROPE_THETA = 10000.0
SG_GROUPS = 4
SG_CHUNK = 128
N_EXPERTS = 16
D_EXPERT = 2048
LN_EPS = 1e-5
QK_EPS = 1e-6
DN_ALPHA = (2 * DEPTH) ** 0.25
QK_SCALE = HEAD_DIM ** -0.5
LOG2E = 1.4426950408889634

LANES = 128
NEG = -1e30
MIB = 1024 * 1024

NA_ROWS_PER_STEP = 4
NA_BLOCKS_PER_STEP = 2
NA_WIN = 12
GQA_TQ = 512
GQA_TK = 512
ROW_TILE = 1024
FFN_TM = 512
FFN_FCHUNK = 512
DMA_UNROLL = 16
ROUTE_PW = 512
COMBINE_TT = 256
COMBINE_RC = 256
COMBINE_STATIC = 3
SMEM_INDEX_BYTES = 128 * 1024


def _cp(semantics, vmem_mib=48):
    return pltpu.CompilerParams(dimension_semantics=semantics, vmem_limit_bytes=vmem_mib * MIB)


def _dot(a, b):
    return jnp.dot(a, b, preferred_element_type=f32)


def _dot_nt(a, b):
    return lax.dot_general(a, b, (((1,), (1,)), ((), ())), preferred_element_type=f32)


def _layer_norm(x, g, b):
    mu = jnp.mean(x, -1, keepdims=True)
    xc = x - mu
    var = jnp.mean(xc * xc, -1, keepdims=True)
    return xc * lax.rsqrt(var + LN_EPS) * g + b


def _split2(x):
    hi = x.astype(bf16)
    lo = (x - hi.astype(f32)).astype(bf16)
    return hi, lo


SLAB = D_MODEL // LANES


def _load_slabs(ref, rows):
    return jnp.concatenate([ref[pl.ds(a, rows, stride=SLAB), :] for a in range(SLAB)], axis=1)


def _store_slabs(ref, x):
    rows = x.shape[0]
    for a in range(SLAB):
        ref[pl.ds(a, rows, stride=SLAB), :] = x[:, a * LANES:(a + 1) * LANES]


def _seg_mean(x, bd):
    hi, lo = _split2(x)
    return _dot(hi, bd) + _dot(lo, bd)


def _ln_in_kernel(x_ref, g_ref, b_ref, o_ref):
    o_ref[...] = _layer_norm(x_ref[...], g_ref[...], b_ref[...])


def _ln_in(x, g, b):
    n = x.shape[0]
    tm = 1024
    row = pl.BlockSpec((tm, D_MODEL), lambda i: (i, 0))
    vec = pl.BlockSpec((1, D_MODEL), lambda i: (0, 0))
    return pl.pallas_call(
        _ln_in_kernel, name="ln_in", grid=(n // tm,), in_specs=[row, vec, vec], out_specs=row,
        out_shape=jax.ShapeDtypeStruct((n, D_MODEL), f32), compiler_params=_cp(("parallel",)),
    )(x, g.reshape(1, -1), b.reshape(1, -1))


def _proj_kernel(x_ref, w_ref, cos_ref, sin_ref, qg_ref, kg_ref, bd_ref,
                 hna_ref, qn_ref, kn_ref, vb_ref, hsg_ref):
    tm = x_ref.shape[0]
    xb = x_ref[...].astype(bf16)

    wide = {}

    def mm(c):
        if c // 2 not in wide:
            wide[c // 2] = _dot(xb, w_ref[:, 512 * (c // 2):512 * (c // 2 + 1)])
        return wide[c // 2][:, 256 * (c % 2):256 * (c % 2 + 1)]

    hna_ref[:, 0:256] = (mm(0) * QK_SCALE).astype(bf16)
    hna_ref[:, 256:512] = mm(1).astype(bf16)
    hna_ref[:, 512:768] = mm(2).astype(bf16)

    cos = cos_ref[...]
    sin = sin_ref[...]
    bd = bd_ref[...]
    even = (lax.broadcasted_iota(i32, (tm, LANES), 1) & 1) == 0

    def norm_rope(x, g):
        y = x * lax.rsqrt(_seg_mean(x * x, bd) + QK_EPS) * g
        partner = jnp.where(even, pltpu.roll(y, LANES - 1, 1), pltpu.roll(y, 1, 1))
        return y * cos + partner * sin

    qg = qg_ref[...]
    for c in (3, 4):
        a = mm(c)
        for j in range(2):
            col = (c - 3) * 256 + j * LANES
            qn_ref[:, col:col + LANES] = (
                norm_rope(a[:, j * LANES:(j + 1) * LANES], qg) * (QK_SCALE * LOG2E)).astype(bf16)
    a = mm(5)
    kn = norm_rope(a[:, :LANES], kg_ref[...]).astype(bf16)
    kn_ref[0, 0] = kn[:, :HEAD_DIM]
    kn_ref[0, 1] = kn[:, HEAD_DIM:]
    vb = a[:, LANES:].astype(bf16)
    ones = jnp.ones((tm, HEAD_DIM), bf16)
    vb_ref[0, 0] = jnp.concatenate([vb[:, :HEAD_DIM], ones], axis=1)
    vb_ref[0, 1] = jnp.concatenate([vb[:, HEAD_DIM:], ones], axis=1)
    hsg_ref[:, :256] = mm(6)
    hsg_ref[:, 256:] = mm(7)


def _proj(x, w, cos_t, sin_t, qg, kg, bd, batch):
    n = x.shape[0]
    tm = ROW_TILE
    per_seq = SEQ // tm
    row = lambda width: pl.BlockSpec((tm, width), lambda i: (i, 0))
    const = lambda shape: pl.BlockSpec(shape, lambda i: tuple(0 for _ in shape))
    tab = pl.BlockSpec((tm, LANES), lambda i: (i % per_seq, 0))
    head_major = lambda width: pl.BlockSpec((1, GQA_KV, tm, width),
                                            lambda i: (i // per_seq, 0, i % per_seq, 0))
    return pl.pallas_call(
        _proj_kernel, name="proj", grid=(n // tm,),
        in_specs=[row(D_MODEL), const(w.shape), tab, tab, const((1, LANES)), const((1, LANES)),
                  const((LANES, LANES))],
        out_specs=[row(768), row(512), head_major(HEAD_DIM), head_major(2 * HEAD_DIM), row(512)],
        out_shape=[jax.ShapeDtypeStruct((n, 768), bf16),
                   jax.ShapeDtypeStruct((n, 512), bf16),
                   jax.ShapeDtypeStruct((batch, GQA_KV, SEQ, HEAD_DIM), bf16),
                   jax.ShapeDtypeStruct((batch, GQA_KV, SEQ, 2 * HEAD_DIM), bf16),
                   jax.ShapeDtypeStruct((n, 512), f32)],
        compiler_params=_cp(("parallel",)),
    )(x, w, cos_t, sin_t, qg, kg, bd)


def _na_window_start(r0):
    return jnp.clip(r0 - NA_KH // 2, 0, GRID_ROWS - NA_WIN)


def _na_kernel(q_ref, k_ref, v_ref, *rest):
    bias_refs, o_ref = rest[:-1], rest[-1]
    tq = NA_ROWS_PER_STEP * GRID_W
    for blk, bias_ref in enumerate(bias_refs):
        r0 = (pl.program_id(1) * NA_BLOCKS_PER_STEP + blk) * NA_ROWS_PER_STEP
        krow = pl.multiple_of(_na_window_start(r0) * GRID_W, GRID_W)
        q = q_ref[0, blk * tq:(blk + 1) * tq, :]
        kw = k_ref[0, pl.ds(krow, NA_WIN * GRID_W), :]
        vw = v_ref[0, pl.ds(krow, NA_WIN * GRID_W), :]
        outs = []
        for h in range(NA_HEADS):
            cs = slice(h * HEAD_DIM, (h + 1) * HEAD_DIM)
            s = _dot_nt(q[:, cs], kw[:, cs]) + bias_ref[0, h]
            m = jnp.max(s, -1, keepdims=True)
            e = jnp.exp(s - m)
            l = jnp.sum(e, -1, keepdims=True)
            outs.append(_dot(e.astype(bf16), vw[:, cs]) / l)
        o_ref[0, blk * tq:(blk + 1) * tq, :] = jnp.concatenate(outs, -1).astype(bf16)


def _na_pattern(block):
    last = GRID_ROWS // NA_ROWS_PER_STEP - 1
    return jnp.where(block == 0, 0, jnp.where(block == last, 2, 1))


def _na(hna, bias_tab):
    batch = hna.shape[0]
    nb = NA_BLOCKS_PER_STEP
    tq = nb * NA_ROWS_PER_STEP * GRID_W
    width = NA_HEADS * HEAD_DIM
    bias_spec = lambda blk: pl.BlockSpec((1,) + bias_tab.shape[1:],
                                         lambda b, i: (_na_pattern(i * nb + blk), 0, 0, 0))
    return pl.pallas_call(
        _na_kernel, name="na", grid=(batch, SEQ // tq),
        in_specs=[pl.BlockSpec((1, tq, width), lambda b, i: (b, i, 0)),
                  pl.BlockSpec((1, SEQ, width), lambda b, i: (b, 0, 1)),
                  pl.BlockSpec((1, SEQ, width), lambda b, i: (b, 0, 2))]
                 + [bias_spec(blk) for blk in range(nb)],
        out_specs=pl.BlockSpec((1, tq, width), lambda b, i: (b, i, 0)),
        out_shape=jax.ShapeDtypeStruct((batch, SEQ, width), bf16),
        compiler_params=_cp(("parallel", "arbitrary")),
    )(hna, hna, hna, *([bias_tab] * nb))


def _na_bias_table(rpb):
    col = np.arange(GRID_W)
    col_start = np.clip(col - NA_KW // 2, 0, GRID_W - NA_KW)
    kc = np.arange(GRID_W)
    col_ok = (kc[None, :] >= col_start[:, None]) & (kc[None, :] < col_start[:, None] + NA_KW)
    dcol = np.clip(kc[None, :] - col[:, None] + NA_KW - 1, 0, 2 * NA_KW - 2)
    pick = jnp.asarray(dcol[None] == np.arange(2 * NA_KW - 1)[:, None, None], f32)
    tabs = []
    for r0 in (0, NA_ROWS_PER_STEP, GRID_ROWS - NA_ROWS_PER_STEP):
        ws = int(np.clip(r0 - NA_KH // 2, 0, GRID_ROWS - NA_WIN))
        r = r0 + np.arange(NA_ROWS_PER_STEP)
        rs = np.clip(r - NA_KH // 2, 0, GRID_ROWS - NA_KH)
        krow = ws + np.arange(NA_WIN)
        row_ok = (krow[None, :] >= rs[:, None]) & (krow[None, :] < rs[:, None] + NA_KH)
        drow = np.clip(krow[None, :] - r[:, None] + NA_KH - 1, 0, 2 * NA_KH - 2)
        g = jnp.einsum("hawk,kcd->hacwd", rpb[:, drow], pick, precision=lax.Precision.HIGHEST)
        ok = row_ok[:, None, :, None] & col_ok[None, :, None, :]
        tabs.append(jnp.where(ok[None], g, NEG).reshape(
            NA_HEADS, NA_ROWS_PER_STEP * GRID_W, NA_WIN * GRID_W))
    return jnp.stack(tabs).astype(f32)


def _gqa_kernel(q_ref, k_ref, v_ref, o_ref):
    g_per_kv = GQA_HEADS // GQA_KV
    q = q_ref[0]
    q4 = jnp.concatenate([q[:, g * HEAD_DIM:(g + 1) * HEAD_DIM] for g in range(g_per_kv)], axis=0)
    rows = g_per_kv * GQA_TQ

    m = jnp.full((rows, 1), NEG, f32)
    acc = jnp.zeros((rows, 2 * HEAD_DIM), f32)
    for j in range(SEQ // GQA_TK):
        kc = k_ref[0, 0, j * GQA_TK:(j + 1) * GQA_TK, :]
        vc = v_ref[0, 0, j * GQA_TK:(j + 1) * GQA_TK, :]
        s = _dot_nt(q4, kc)
        mn = jnp.maximum(m, jnp.max(s, -1, keepdims=True))
        acc = jnp.exp2(m - mn) * acc + _dot(jnp.exp2(s - mn).astype(bf16), vc)
        m = mn
    o = acc[:, :HEAD_DIM] / acc[:, HEAD_DIM:HEAD_DIM + 1]
    for g in range(g_per_kv):
        o_ref[0, :, g * HEAD_DIM:(g + 1) * HEAD_DIM] = o[g * GQA_TQ:(g + 1) * GQA_TQ].astype(bf16)


def _gqa(qn, kn, vb):
    batch = qn.shape[0]
    width = (GQA_HEADS // GQA_KV) * HEAD_DIM
    qspec = pl.BlockSpec((1, GQA_TQ, width), lambda b, kv, i: (b, i, kv))
    kvspec = lambda width: pl.BlockSpec((1, 1, SEQ, width), lambda b, kv, i: (b, kv, 0, 0))
    return pl.pallas_call(
        _gqa_kernel, name="gqa", grid=(batch, GQA_KV, SEQ // GQA_TQ),
        in_specs=[qspec, kvspec(HEAD_DIM), kvspec(2 * HEAD_DIM)], out_specs=qspec,
        out_shape=jax.ShapeDtypeStruct((batch, SEQ, GQA_HEADS * HEAD_DIM), bf16),
        compiler_params=_cp(("parallel", "parallel", "arbitrary")),
    )(qn, kn, vb)


def _sgu_kernel(h_ref, ws_ref, bs_ref, g_ref, b_ref, bd_ref, o_ref):
    tm = h_ref.shape[0]
    width = SG_GROUPS * HEAD_DIM
    bd = bd_ref[...]
    u = jax.nn.gelu(h_ref[:, :width])
    v = jax.nn.gelu(h_ref[:, width:])
    parts = []
    for j in range(width // LANES):
        x = v[:, j * LANES:(j + 1) * LANES]
        xc = x - _seg_mean(x, bd)
        parts.append(xc * lax.rsqrt(_seg_mean(xc * xc, bd) + LN_EPS))
    vn = (jnp.concatenate(parts, 1) * g_ref[...] + b_ref[...]).astype(bf16)
    bs = bs_ref[...]
    for c in range(tm // SG_CHUNK):
        rows = slice(c * SG_CHUNK, (c + 1) * SG_CHUNK)
        mixed = jnp.concatenate(
            [_dot(ws_ref[g], vn[rows, g * HEAD_DIM:(g + 1) * HEAD_DIM]) for g in range(SG_GROUPS)], axis=1)
        o_ref[rows, :] = (u[rows] * (mixed + bs)).astype(bf16)


def _sgu(hsg, ws, bs_tab, g, b, bd):
    n = hsg.shape[0]
    tm = ROW_TILE
    width = SG_GROUPS * HEAD_DIM
    const = lambda shape: pl.BlockSpec(shape, lambda i: tuple(0 for _ in shape))
    return pl.pallas_call(
        _sgu_kernel, name="sgu", grid=(n // tm,),
        in_specs=[pl.BlockSpec((tm, 2 * width), lambda i: (i, 0)), const(ws.shape), const(bs_tab.shape),
                  const((1, width)), const((1, width)), const((LANES, LANES))],
        out_specs=pl.BlockSpec((tm, width), lambda i: (i, 0)),
        out_shape=jax.ShapeDtypeStruct((n, width), bf16),
        compiler_params=_cp(("parallel",)),
    )(hsg, ws, bs_tab, g, b, bd)


def _outproj_kernel(ya_ref, yb_ref, yc_ref, x_ref, w_ref, g_ref, b_ref, wrh_ref, wrl_ref, x1_ref, aff_ref):
    mix = (_dot(ya_ref[...], w_ref[0:256]) + _dot(yb_ref[...], w_ref[256:768])
           + _dot(yc_ref[...], w_ref[768:1024]))
    x1 = _layer_norm(DN_ALPHA * x_ref[...] + mix, g_ref[...], b_ref[...])
    _store_slabs(x1_ref, x1)
    hi, lo = _split2(x1)
    both = _dot_nt(wrl_ref[...], hi)
    logits = both[:N_EXPERTS] + both[N_EXPERTS:] + _dot_nt(wrh_ref[...], lo)
    m = jnp.max(logits, 0, keepdims=True)
    e = jnp.exp(logits - m)
    aff_ref[...] = e / jnp.sum(e, 0, keepdims=True)


def _outproj(ya, yb, yc, x, w, g, b, wrh, wrl):
    n = x.shape[0]
    tm = ROW_TILE
    row = lambda width: pl.BlockSpec((tm, width), lambda i: (i, 0))
    const = lambda shape: pl.BlockSpec(shape, lambda i: tuple(0 for _ in shape))
    return pl.pallas_call(
        _outproj_kernel, name="outproj", grid=(n // tm,),
        in_specs=[row(256), row(512), row(256), row(D_MODEL), const(w.shape), const((1, D_MODEL)),
                  const((1, D_MODEL)), const(wrh.shape), const(wrl.shape)],
        out_specs=[pl.BlockSpec((tm * SLAB, LANES), lambda i: (i, 0)),
                   pl.BlockSpec((N_EXPERTS, tm), lambda i: (0, i))],
        out_shape=[jax.ShapeDtypeStruct((n * SLAB, LANES), f32), jax.ShapeDtypeStruct((N_EXPERTS, n), f32)],
        compiler_params=_cp(("parallel",)),
    )(ya, yb, yc, x, w, g, b, wrh, wrl)


def _tri_incl():
    r = lax.broadcasted_iota(i32, (LANES, LANES), 0)
    c = lax.broadcasted_iota(i32, (LANES, LANES), 1)
    return (r <= c).astype(bf16)


def _tri_strict_lower(nc):
    r = lax.broadcasted_iota(i32, (nc, nc), 0)
    c = lax.broadcasted_iota(i32, (nc, nc), 1)
    return (c < r).astype(bf16)


def _byte_planes(x, planes):
    xi = x.astype(i32)
    return [((xi >> (8 * k)) & 255).astype(f32).astype(bf16) for k in range(planes)]


def _token_prefix(m, planes):
    nc = m.shape[0]
    lincl = _dot(m.astype(bf16), _tri_incl())
    tot = jnp.broadcast_to(lincl[:, LANES - 1:LANES], m.shape)
    sl = _tri_strict_lower(nc)
    pref = sum(_dot(sl, p) * float(256 ** k) for k, p in enumerate(_byte_planes(tot, planes)))
    return lincl, tot, pref


def _route1_kernel(aff_ref, sel_ref, *, cap):
    bits = lax.bitcast_convert_type(aff_ref[0], i32)

    def enough(cand):
        return jnp.sum((bits >= cand).astype(f32)) >= cap

    def body(i, t):
        one = lax.shift_left(jnp.int32(1), 28 - 2 * i)
        c1, c2, c3 = t | one, t | (one << 1), t | (one << 1) | one
        return jnp.where(enough(c3), c3, jnp.where(enough(c2), c2, jnp.where(enough(c1), c1, t)))

    top = jnp.int32(1 << 30)
    thr = lax.fori_loop(0, 15, body, jnp.where(enough(top), top, 0))
    gt = bits > thr
    eq = bits == thr
    need = cap - jnp.sum(gt.astype(f32))
    eqf = eq.astype(f32)
    lincl, _, pref = _token_prefix(eqf, 1)
    before = pref + lincl - eqf
    sel_ref[0] = (gt | (eq & (before < need))).astype(f32)


def _route1(aff_t, cap):
    ne, nc, _ = aff_t.shape
    spec = pl.BlockSpec((1, nc, LANES), lambda e: (e, 0, 0))
    return pl.pallas_call(
        functools.partial(_route1_kernel, cap=cap), name="route1", grid=(ne,), in_specs=[spec],
        out_specs=spec,
        out_shape=jax.ShapeDtypeStruct(aff_t.shape, f32), compiler_params=_cp(("parallel",)),
    )(aff_t)


def _route2_kernel(aff_ref, sel_ref, idx_ref, gate_ref, rho_ref, off_ref, cnt_ref, *, cap):
    e = pl.program_id(0)
    nc = aff_ref.shape[1]
    pw = ROUTE_PW

    count = jnp.zeros((nc, LANES), f32)
    rank = jnp.zeros((nc, LANES), f32)
    for k in range(N_EXPERTS):
        sk = sel_ref[k]
        count = count + sk
        rank = rank + jnp.where(k < e, sk, 0.0)
    lc, _, pc = _token_prefix(count, 2)
    off = pc + lc - count
    off_ref[...] = off
    cnt_ref[...] = count

    sel = sel_ref[e]
    lincl, tot, pref = _token_prefix(sel, 1)
    cse = pref[:, 0:1]
    csi = cse + tot[:, 0:1]
    chunk_id = lax.broadcasted_iota(i32, (nc, 1), 0).astype(f32)

    lt = lincl.T.astype(bf16)
    a = aff_ref[0].T
    a1 = a.astype(bf16)
    r1 = a - a1.astype(f32)
    a2 = r1.astype(bf16)
    a3 = (r1 - a2.astype(f32)).astype(bf16)
    dest = _byte_planes((off + rank).T, 3)
    tables = jnp.concatenate([lt, a1, a2, a3] + dest, axis=0)
    lane_id = lax.broadcasted_iota(i32, (LANES, pw), 0).astype(f32)

    def tile(k, carry):
        p = (k * pw + lax.broadcasted_iota(i32, (1, pw), 1)).astype(f32)
        oh = (cse <= p) & (p < csi)
        ohb = oh.astype(f32).astype(bf16)
        chunk = jnp.sum(jnp.where(oh, chunk_id, 0.0), axis=0, keepdims=True)
        local = p - jnp.sum(jnp.where(oh, cse, 0.0), axis=0, keepdims=True)
        looked = _dot(tables, ohb)
        part = lambda t: looked[t * LANES:(t + 1) * LANES]
        jstar = jnp.sum((part(0) <= local).astype(f32), axis=0, keepdims=True)
        hit = lane_id == jstar
        g = part(1) + part(2) + part(3)
        r = part(4) + 256.0 * part(5) + 65536.0 * part(6)
        idx_ref[0, pl.ds(k, 1), :] = (chunk * LANES + jstar).astype(i32)
        gate_ref[0, pl.ds(k, 1), :] = jnp.sum(jnp.where(hit, g, 0.0), axis=0, keepdims=True)
        rho_ref[0, pl.ds(k, 1), :] = jnp.sum(jnp.where(hit, r, 0.0), axis=0, keepdims=True).astype(i32)
        return carry

    lax.fori_loop(0, cap // pw, tile, 0)


def _route2(aff_t, sel, cap):
    ne, nc, _ = aff_t.shape
    pw = ROUTE_PW
    slot = pl.BlockSpec((1, cap // pw, pw), lambda e: (e, 0, 0))
    tok = pl.BlockSpec((nc, LANES), lambda e: (0, 0))
    slot_shape = lambda dt: jax.ShapeDtypeStruct((ne, cap // pw, pw), dt)
    return pl.pallas_call(
        functools.partial(_route2_kernel, cap=cap), name="route2", grid=(ne,),
        in_specs=[pl.BlockSpec((1, nc, LANES), lambda e: (e, 0, 0)),
                  pl.BlockSpec((ne, nc, LANES), lambda e: (0, 0, 0))],
        out_specs=[slot, slot, slot, tok, tok],
        out_shape=[slot_shape(i32), slot_shape(f32), slot_shape(i32),
                   jax.ShapeDtypeStruct((nc, LANES), f32), jax.ShapeDtypeStruct((nc, LANES), f32)],
        compiler_params=_cp(("arbitrary",)),
    )(aff_t, sel)


def _row_to_col(row):
    n = row.shape[1]
    r = lax.broadcasted_iota(i32, (n, n), 0)
    c = lax.broadcasted_iota(i32, (n, n), 1)
    return jnp.sum(jnp.where(r == c, row, 0.0), axis=1, keepdims=True)


def _ffn_kernel(idx_sm, rho_sm, x_hbm, gate_ref, wg_ref, wu_ref, wd_ref, *rest, aliased):
    if aliased:
        _, z_hbm, xbuf, ybuf, gsem, ssem = rest
    else:
        z_hbm, xbuf, ybuf, gsem, ssem = rest
    tm = FFN_TM
    mt = pl.num_programs(1)
    total = pl.num_programs(0) * mt
    s = pl.program_id(0) * mt + pl.program_id(1)
    slot = s % 2

    def gather_copy(tok, i, sl):
        return pltpu.make_async_copy(x_hbm.at[pl.ds(pl.multiple_of(tok * SLAB, SLAB), SLAB), :],
                                     xbuf.at[sl, pl.ds(pl.multiple_of(i * SLAB, SLAB), SLAB), :], gsem.at[sl])

    def scatter_copy(row, i, sl):
        return pltpu.make_async_copy(ybuf.at[sl, pl.ds(pl.multiple_of(i * SLAB, SLAB), SLAB), :],
                                     z_hbm.at[pl.ds(pl.multiple_of(row * SLAB, SLAB), SLAB), :], ssem.at[sl])

    def start_gather(step, sl):
        def body(i, c):
            gather_copy(idx_sm[step * tm + i], i, sl).start()
            return c
        lax.fori_loop(0, tm, body, 0, unroll=DMA_UNROLL)

    def wait_gather(sl):
        pltpu.make_async_copy(x_hbm.at[pl.ds(0, tm * SLAB), :], xbuf.at[sl], gsem.at[sl]).wait()

    def wait_scatter(sl):
        pltpu.make_async_copy(ybuf.at[sl], z_hbm.at[pl.ds(0, tm * SLAB), :], ssem.at[sl]).wait()

    @pl.when(s == 0)
    def _():
        start_gather(0, 0)

    @pl.when(s + 1 < total)
    def _():
        start_gather(s + 1, 1 - slot)

    wait_gather(slot)

    @pl.when(s >= 2)
    def _():
        wait_scatter(slot)

    x = _load_slabs(xbuf.at[slot], tm).astype(bf16)
    y = jnp.zeros((tm, D_MODEL), f32)
    for c in range(D_EXPERT // FFN_FCHUNK):
        cols = slice(c * FFN_FCHUNK, (c + 1) * FFN_FCHUNK)
        hg = _dot(x, wg_ref[0, 0, :, cols])
        hu = _dot(x, wu_ref[0, 0, :, cols])
        y = y + _dot((jax.nn.silu(hg) * hu).astype(bf16), wd_ref[0, 0, cols, :])
    _store_slabs(ybuf.at[slot], y * _row_to_col(gate_ref[0]))

    def start_scatter(i, c):
        scatter_copy(rho_sm[s * tm + i], i, slot).start()
        return c
    lax.fori_loop(0, tm, start_scatter, 0, unroll=DMA_UNROLL)

    @pl.when(s == total - 1)
    def _():
        wait_scatter(slot)

        @pl.when(total >= 2)
        def _():
            wait_scatter(1 - slot)


def _ffn_call(idx, rho, x1, gate, wg, wu, wd, layer, z, e0, eg, cap):
    tm = FFN_TM
    mt = cap // tm
    n_rows = N_EXPERTS * cap
    aliased = z is not None
    any_spec = pl.BlockSpec(memory_space=pl.ANY)
    in_specs = [any_spec,
                pl.BlockSpec((1, 1, tm), lambda e, m, *_: ((e0 + e) * mt + m, 0, 0)),
                pl.BlockSpec((1, 1, D_MODEL, D_EXPERT), lambda e, m, *_: (layer, e0 + e, 0, 0)),
                pl.BlockSpec((1, 1, D_MODEL, D_EXPERT), lambda e, m, *_: (layer, e0 + e, 0, 0)),
                pl.BlockSpec((1, 1, D_EXPERT, D_MODEL), lambda e, m, *_: (layer, e0 + e, 0, 0))]
    args = [x1, gate, wg, wu, wd]
    if aliased:
        in_specs.append(any_spec)
        args.append(z)
    grid_spec = pltpu.PrefetchScalarGridSpec(
        num_scalar_prefetch=2, grid=(eg, mt), in_specs=in_specs, out_specs=any_spec,
        scratch_shapes=[pltpu.VMEM((2, tm * SLAB, LANES), f32), pltpu.VMEM((2, tm * SLAB, LANES), f32),
                        pltpu.SemaphoreType.DMA((2,)), pltpu.SemaphoreType.DMA((2,))])
    return pl.pallas_call(
        functools.partial(_ffn_kernel, aliased=aliased), name="ffn", grid_spec=grid_spec,
        out_shape=jax.ShapeDtypeStruct((n_rows * SLAB, LANES), f32),
        input_output_aliases={7: 0} if aliased else {},
        compiler_params=pltpu.CompilerParams(dimension_semantics=("arbitrary", "arbitrary"),
                                             vmem_limit_bytes=56 * MIB, has_side_effects=True),
    )(idx[e0:e0 + eg].reshape(-1), rho[e0:e0 + eg].reshape(-1), *args)


def _ffn(idx, rho, x1, gate, wg, wu, wd, layer, cap):
    eg = max(1, min(N_EXPERTS, SMEM_INDEX_BYTES // (2 * 4 * cap)))
    gate3 = gate.reshape(N_EXPERTS * cap // FFN_TM, 1, FFN_TM)
    z = None
    for e0 in range(0, N_EXPERTS, eg):
        z = _ffn_call(idx, rho, x1, gate3, wg, wu, wd, layer, z, e0, eg, cap)
    return z


def _combine_kernel(rs_sm, x_ref, off_ref, cnt_ref, g_ref, b_ref, z_hbm, o_ref, zbuf, sem, *, total_rows):
    tt = COMBINE_TT
    rc = COMBINE_RC
    i = pl.program_id(0)

    def tile_rows(t):
        first = (rs_sm[t] >> 3) << 3
        return first, (rs_sm[t + 1] - first + rc - 1) >> (rc.bit_length() - 1)

    r0, nch = tile_rows(i)

    offs = []
    ends = []
    for c in range(tt // LANES):
        o_row = off_ref[0, c:c + 1, :]
        offs.append(_row_to_col(o_row))
        ends.append(_row_to_col(o_row + cnt_ref[0, c:c + 1, :]))
    off_col = jnp.concatenate(offs, 0)
    end_col = jnp.concatenate(ends, 0)

    def chunk_start(first, j):
        return pl.multiple_of(jnp.minimum(first + j * rc, total_rows - rc), 8)

    def chunk_copy(first, j, sl):
        return pltpu.make_async_copy(z_hbm.at[pl.ds(chunk_start(first, j) * SLAB, rc * SLAB), :], zbuf.at[sl],
                                     sem.at[sl])

    def contribution(j, sl):
        rowid = chunk_start(r0, j) + lax.broadcasted_iota(i32, (1, rc), 1)
        fresh = rowid >= r0 + j * rc
        rowf = rowid.astype(f32)
        seg = ((off_col <= rowf) & (rowf < end_col) & fresh).astype(f32).astype(bf16)
        hi, lo = _split2(_load_slabs(zbuf.at[sl], rc))
        return _dot(seg, hi) + _dot(seg, lo)

    last = pl.num_programs(0) - 1
    nxt, _ = tile_rows(jnp.minimum(i + 1, last))

    cur = (i % 2) * COMBINE_STATIC
    oth = COMBINE_STATIC - cur

    @pl.when(i == 0)
    def _():
        for j in range(COMBINE_STATIC):
            chunk_copy(r0, j, cur + j).start()

    for j in range(COMBINE_STATIC):
        chunk_copy(r0, j, cur + j).wait()
    for j in range(COMBINE_STATIC):
        chunk_copy(nxt, j, oth + j).start()

    acc = jnp.zeros((tt, D_MODEL), f32)
    for j in range(COMBINE_STATIC):
        acc = acc + contribution(j, cur + j)

    def tail(j, acc):
        cp = chunk_copy(r0, j, 2 * COMBINE_STATIC)
        cp.start()
        cp.wait()
        return acc + contribution(j, 2 * COMBINE_STATIC)

    acc = lax.fori_loop(COMBINE_STATIC, jnp.maximum(nch, COMBINE_STATIC), tail, acc)

    @pl.when(i == last)
    def _():
        for j in range(COMBINE_STATIC):
            chunk_copy(nxt, j, oth + j).wait()

    o_ref[...] = _layer_norm(DN_ALPHA * _load_slabs(x_ref, tt) + acc, g_ref[...], b_ref[...])


def _combine(row_starts, x1, off, cnt, g, b, z):
    n = x1.shape[0] // SLAB
    tt = COMBINE_TT
    per = tt // LANES
    tok = pl.BlockSpec((1, per, LANES), lambda i, *_: (i, 0, 0))
    vec = pl.BlockSpec((1, D_MODEL), lambda i, *_: (0, 0))
    row = pl.BlockSpec((tt, D_MODEL), lambda i, *_: (i, 0))
    grid_spec = pltpu.PrefetchScalarGridSpec(
        num_scalar_prefetch=1, grid=(n // tt,),
        in_specs=[pl.BlockSpec((tt * SLAB, LANES), lambda i, *_: (i, 0)), tok, tok, vec, vec,
                  pl.BlockSpec(memory_space=pl.ANY)], out_specs=row,
        scratch_shapes=[pltpu.VMEM((2 * COMBINE_STATIC + 1, COMBINE_RC * SLAB, LANES), f32),
                        pltpu.SemaphoreType.DMA((2 * COMBINE_STATIC + 1,))])
    return pl.pallas_call(
        functools.partial(_combine_kernel, total_rows=z.shape[0] // SLAB), name="combine",
        grid_spec=grid_spec,
        out_shape=jax.ShapeDtypeStruct((n, D_MODEL), f32), compiler_params=_cp(("arbitrary",)),
    )(row_starts, x1, off.reshape(n // tt, per, LANES), cnt.reshape(n // tt, per, LANES), g, b, z)


def _rope_tables():
    t = np.arange(SEQ)
    row = (t // GRID_W).astype(np.float32)
    col = (t % GRID_W).astype(np.float32)
    n_freq = HEAD_DIM // 4
    inv = jnp.asarray(ROPE_THETA, f32) ** (-jnp.arange(n_freq, dtype=f32) / n_freq)
    ang = jnp.concatenate([row[:, None] * inv, col[:, None] * inv], -1)
    cos = jnp.repeat(jnp.cos(ang), 2, axis=1)
    sin = jnp.repeat(jnp.sin(ang), 2, axis=1)
    sign = jnp.tile(jnp.asarray([-1.0, 1.0], f32), HEAD_DIM // 2)
    return jnp.tile(cos, (1, 2)), jnp.tile(sin * sign, (1, 2))


def _seg_matrix():
    r = np.arange(LANES)
    return jnp.asarray((r[:, None] // HEAD_DIM == r[None, :] // HEAD_DIM) / HEAD_DIM, bf16)


def _layer(x, batch, p, cos_t, sin_t, bd):
    n = x.shape[0]
    cap = 2 * n // N_EXPERTS
    hna, qn, kn, vb, hsg = _proj(x, p["w_in"], cos_t, sin_t, p["q_norm"], p["k_norm"], bd, batch)
    ya = _na(hna.reshape(batch, SEQ, -1), p["na_bias"]).reshape(n, -1)
    yb = _gqa(qn.reshape(batch, SEQ, -1), kn, vb).reshape(n, -1)
    yc = _sgu(hsg, p["sg_w"], p["sg_b"], p["sg_ln_g"], p["sg_ln_b"], bd)
    x1, aff = _outproj(ya, yb, yc, x, p["w_out"], p["ln1_g"], p["ln1_b"], p["wr_hi"], p["wr_lo"])
    aff_t = aff.reshape(N_EXPERTS, n // LANES, LANES)
    sel = _route1(aff_t, cap)
    idx, gate, rho, off, cnt = _route2(aff_t, sel, cap)
    z = _ffn(idx.reshape(N_EXPERTS, cap), rho.reshape(N_EXPERTS, cap), x1, gate,
             p["w_gate"], p["w_up"], p["w_down"], p["layer"], cap)
    row_starts = jnp.concatenate(
        [off.reshape(-1)[::COMBINE_TT], jnp.full((1,), z.shape[0] // SLAB, f32)]).astype(i32)
    return _combine(row_starts, x1, off, cnt, p["ln2_g"], p["ln2_b"], z)


def _trunk(x, ln_in_g, ln_in_b, layers, cos_t, sin_t, bd):
    batch = x.shape[0]
    h = _ln_in(x.reshape(batch * SEQ, D_MODEL), ln_in_g, ln_in_b)
    for p in layers:
        h = _layer(h, batch, p, cos_t, sin_t, bd)
    return h.reshape(batch, SEQ, D_MODEL)


def _layer_params(l, w_in, na_rpb, q_norm, k_norm, sg_w, sg_b, sg_ln_g, sg_ln_b, w_out, ln1_g, ln1_b,
                  w_router, w_gate, w_up, w_down, ln2_g, ln2_b):
    wr = w_router[l]
    wr_hi = wr.astype(bf16)
    width = SG_GROUPS * HEAD_DIM
    return {
        "w_in": w_in[l].astype(bf16),
        "na_bias": _na_bias_table(na_rpb[l]),
        "q_norm": jnp.tile(q_norm[l], 2).reshape(1, LANES),
        "k_norm": jnp.tile(k_norm[l], 2).reshape(1, LANES),
        "sg_w": sg_w[l].astype(bf16),
        "sg_b": jnp.repeat(sg_b[l].T, HEAD_DIM, axis=1),
        "sg_ln_g": sg_ln_g[l].reshape(1, width),
        "sg_ln_b": sg_ln_b[l].reshape(1, width),
        "w_out": w_out[l].astype(bf16),
        "ln1_g": ln1_g[l].reshape(1, -1), "ln1_b": ln1_b[l].reshape(1, -1),
        "wr_hi": wr_hi.T,
        "wr_lo": jnp.concatenate([wr_hi, (wr - wr_hi.astype(f32)).astype(bf16)], axis=1).T,
        "w_gate": w_gate, "w_up": w_up, "w_down": w_down, "layer": l,
        "ln2_g": ln2_g[l].reshape(1, -1), "ln2_b": ln2_b[l].reshape(1, -1),
    }


def kernel(x_prompt, x_sample, ln_in_g, ln_in_b, w_in, na_rpb, q_norm, k_norm, sg_w, sg_b, sg_ln_g, sg_ln_b,
           w_out, ln1_g, ln1_b, w_router, w_gate, w_up, w_down, ln2_g, ln2_b):
    w_gate, w_up, w_down = (w.astype(bf16) for w in (w_gate, w_up, w_down))
    layers = [_layer_params(l, w_in, na_rpb, q_norm, k_norm, sg_w, sg_b, sg_ln_g, sg_ln_b, w_out, ln1_g,
                            ln1_b, w_router, w_gate, w_up, w_down, ln2_g, ln2_b)
              for l in range(w_in.shape[0])]
    cos_t, sin_t = _rope_tables()
    bd = _seg_matrix()
    return (_trunk(x_prompt, ln_in_g, ln_in_b, layers, cos_t, sin_t, bd),
            _trunk(x_sample, ln_in_g, ln_in_b, layers, cos_t, sin_t, bd))
```
